```python
import jax, jax.numpy as jnp
from jax import lax
import numpy as np

D_MODEL = 1024
BATCH = 4
SEQ = 4096
DEPTH = 4
DEC_BATCH = 16
DEC_SEQ = 64
PAST_LEN = 4096

CHUNK = 64
HEAD_DIM = 64
FOX_HEADS = D_MODEL // HEAD_DIM
FOX_QBLOCK = 128
SWA_HEADS = D_MODEL // HEAD_DIM
SWA_KV_HEADS = SWA_HEADS // 4
SWA_GROUP = SWA_HEADS // SWA_KV_HEADS
WINDOW = 128
WINDOW_CHUNKS = WINDOW // CHUNK
BAND = (WINDOW_CHUNKS + 1) * CHUNK
ROPE_THETA = 10000.0
N_GROUPS = 4
EXPERTS_PER_GROUP = 4
N_EXPERTS = N_GROUPS * EXPERTS_PER_GROUP
TOP_K_INNER = 2
D_EXPERT = D_MODEL // 4
N_FOX = (DEPTH + 1) // 2
N_SWA = DEPTH // 2
DEEPNORM_ALPHA = (2.0 * DEPTH) ** 0.25
DEEPNORM_BETA = (8.0 * DEPTH) ** -0.25
LN_EPS = 1e-5
FOX_IN = 3 * FOX_HEADS * HEAD_DIM + FOX_HEADS
SWA_IN = (SWA_HEADS + 2 * SWA_KV_HEADS) * HEAD_DIM
ATT_SCALE = HEAD_DIM ** -0.5

kernel_name = "fox_swa_sink_hmoe_streaming_step"

F32 = jnp.float32


def layer_norm(x, g, b):
    xf = x.astype(F32)
    mu = jnp.mean(xf, axis=-1, keepdims=True)
    var = jnp.mean(jnp.square(xf - mu), axis=-1, keepdims=True)
    return ((xf - mu) * lax.rsqrt(var + LN_EPS) * g.astype(F32) + b.astype(F32)).astype(x.dtype)


def rope(x, pos):
    half = HEAD_DIM // 2
    inv = ROPE_THETA ** (-jnp.arange(half, dtype=F32) / half)
    ang = pos.astype(F32)[:, None] * inv[None, :]
    cos = jnp.cos(ang)[None, :, None, :]
    sin = jnp.sin(ang)[None, :, None, :]
    xf = x.astype(F32)
    x1, x2 = xf[..., :half], xf[..., half:]
    return jnp.concatenate([x1 * cos - x2 * sin, x2 * cos + x1 * sin], axis=-1).astype(x.dtype)


def modulation(c, w, b):
    m = jax.nn.silu(c) @ w + b
    return jnp.split(m[:, None, :], 6, axis=-1)


def deepnorm_residual(x, out, gate, g, b):
    return layer_norm(DEEPNORM_ALPHA * x + (1.0 + gate) * out, g, b)


def fox_project(h, w_in, b_in):
    B, L, _ = h.shape
    p = h @ w_in + b_in
    n = FOX_HEADS * HEAD_DIM
    q = p[..., :n].reshape(B, L, FOX_HEADS, HEAD_DIM)
    k = p[..., n:2 * n].reshape(B, L, FOX_HEADS, HEAD_DIM)
    v = p[..., 2 * n:3 * n].reshape(B, L, FOX_HEADS, HEAD_DIM)
    logf = jax.nn.log_sigmoid(p[..., 3 * n:].astype(F32))
    return q, k, v, logf


def fox_attend(q, k, v, fq, fk, qpos, kpos):
    s = jnp.einsum('bqhd,bkhd->bhqk', q, k).astype(F32) * ATT_SCALE
    s = s + jnp.swapaxes(fq, 1, 2)[..., :, None] + jnp.swapaxes(fk, 1, 2)[..., None, :]
    s = jnp.where(kpos[None, :] <= qpos[:, None], s, -jnp.inf)
    p = jax.nn.softmax(s, axis=-1)
    return jnp.einsum('bhqk,bkhd->bqhd', p.astype(v.dtype), v)


def fox_prompt(h, w_in, b_in, w_out):
    B, S, _ = h.shape
    q, k, v, logf = fox_project(h, w_in, b_in)
    cum = jnp.cumsum(logf, axis=1)
    kpos = jnp.arange(S)

    def block(i):
        start = i * FOX_QBLOCK
        qb = lax.dynamic_slice_in_dim(q, start, FOX_QBLOCK, axis=1)
        fq = lax.dynamic_slice_in_dim(cum, start, FOX_QBLOCK, axis=1)
        qpos = start + jnp.arange(FOX_QBLOCK)
        return fox_attend(qb, k, v, fq, -cum, qpos, kpos)

    o = lax.map(block, jnp.arange(S // FOX_QBLOCK))
    o = jnp.moveaxis(o, 0, 1).reshape(B, S, FOX_HEADS * HEAD_DIM)
    return o @ w_out, k, v, logf


def fox_sample(h, ck, cv, clogf, w_in, b_in, w_out):
    B, T, _ = h.shape
    P = ck.shape[1]
    q, k, v, logf = fox_project(h, w_in, b_in)
    cl = clogf.astype(F32)
    suffix = lax.cumsum(cl, axis=1, reverse=True) - cl
    fn = jnp.cumsum(logf, axis=1)
    fk = jnp.concatenate([suffix, -fn], axis=1)
    kk = jnp.concatenate([ck, k], axis=1)
    vv = jnp.concatenate([cv, v], axis=1)
    o = fox_attend(q, kk, vv, fn, fk, P + jnp.arange(T), jnp.arange(P + T))
    return o.reshape(B, T, FOX_HEADS * HEAD_DIM) @ w_out, k, v, logf


def swa_project(h, w_in, b_in, pos):
    B, L, _ = h.shape
    p = h @ w_in + b_in
    nq = SWA_HEADS * HEAD_DIM
    nkv = SWA_KV_HEADS * HEAD_DIM
    q = rope(p[..., :nq].reshape(B, L, SWA_HEADS, HEAD_DIM), pos)
    k = rope(p[..., nq:nq + nkv].reshape(B, L, SWA_KV_HEADS, HEAD_DIM), pos)
    v = p[..., nq + nkv:].reshape(B, L, SWA_KV_HEADS, HEAD_DIM)
    return q, k, v


def sink_softmax(s, sinks):
    sk = sinks.astype(F32).reshape(SWA_KV_HEADS, SWA_GROUP)[:, :, None, None]
    m = jnp.maximum(jnp.max(s, axis=-1, keepdims=True), sk)
    p = jnp.exp(s - m)
    return p / (jnp.sum(p, axis=-1, keepdims=True) + jnp.exp(sk - m))


def swa_prompt(h, w_in, b_in, sinks, w_out):
    B, S, _ = h.shape
    nc = S // CHUNK
    q, k, v = swa_project(h, w_in, b_in, jnp.arange(S))

    def bands(t):
        tp = jnp.pad(t, ((0, 0), (WINDOW, 0), (0, 0), (0, 0)))
        tp = tp.reshape(B, nc + WINDOW_CHUNKS, CHUNK, SWA_KV_HEADS, HEAD_DIM)
        return jnp.concatenate([tp[:, j:j + nc] for j in range(WINDOW_CHUNKS + 1)], axis=2)

    kb, vb = bands(k), bands(v)
    kpos = jnp.arange(nc)[:, None] * CHUNK - WINDOW + jnp.arange(BAND)[None, :]
    qb = q.reshape(B, nc, CHUNK, SWA_KV_HEADS, SWA_GROUP, HEAD_DIM)
    s = jnp.einsum('bcqkgd,bcskd->bckgqs', qb, kb).astype(F32) * ATT_SCALE
    s = jnp.where((kpos >= 0)[None, :, None, None, None, :], s, -jnp.inf)
    p = sink_softmax(s, sinks)
    o = jnp.einsum('bckgqs,bcskd->bcqkgd', p.astype(vb.dtype), vb).reshape(B, S, SWA_HEADS * HEAD_DIM)
    return o @ w_out, k[:, S - WINDOW:], v[:, S - WINDOW:]


def swa_sample(h, ck, cv, past_len, w_in, b_in, sinks, w_out):
    B, T, _ = h.shape
    keep = ck.shape[1]
    q, k, v = swa_project(h, w_in, b_in, past_len + jnp.arange(T))
    kk = jnp.concatenate([ck, k], axis=1)
    vv = jnp.concatenate([cv, v], axis=1)
    qg = q.reshape(B, T, SWA_KV_HEADS, SWA_GROUP, HEAD_DIM)
    s = jnp.einsum('bqkgd,bskd->bkgqs', qg, kk).astype(F32) * ATT_SCALE
    p = sink_softmax(s, sinks)
    o = jnp.einsum('bkgqs,bskd->bqkgd', p.astype(vv.dtype), vv).reshape(B, T, SWA_HEADS * HEAD_DIM)
    return o @ w_out, kk[:, kk.shape[1] - keep:], vv[:, vv.shape[1] - keep:]


def hier_moe(h, rg_w, rg_b, re_w, re_b, w_gate, w_up, w_down):
    B, L, D = h.shape
    t = h.reshape(B * L, D)
    n = t.shape[0]
    gprob = jax.nn.softmax((t @ rg_w + rg_b).astype(F32), axis=-1)
    gp, gi = lax.top_k(gprob, 1)
    elog = jnp.einsum('nd,gde->nge', t, re_w) + re_b
    elog = elog[jnp.arange(n), gi[:, 0]]
    eprob = jax.nn.softmax(elog.astype(F32), axis=-1)
    ep, ei = lax.top_k(eprob, TOP_K_INNER)
    w = gp * ep / jnp.sum(ep, axis=-1, keepdims=True)
    ids = gi * EXPERTS_PER_GROUP + ei
    gates = jnp.sum(jax.nn.one_hot(ids, N_EXPERTS, dtype=F32) * w[..., None], axis=1)
    hg = jnp.einsum('nd,edf->nef', t, w_gate)
    hu = jnp.einsum('nd,edf->nef', t, w_up)
    a = jax.nn.silu(hg) * hu * gates[..., None].astype(t.dtype)
    return jnp.einsum('nef,efd->nd', a, w_down).reshape(B, L, D)


def setup_inputs(seed: int = 0) -> dict:
    key = jax.random.key(seed)
    ks = jax.random.split(key, 32)
    nrm = jax.random.normal
    swa_keep = min(WINDOW, PAST_LEN)
    d_in = D_MODEL ** -0.5
    return {
        "x_prompt": nrm(ks[0], (BATCH, SEQ, D_MODEL), F32),
        "x_sample": nrm(ks[1], (DEC_BATCH, DEC_SEQ, D_MODEL), F32),
        "cache_fox_k": nrm(ks[2], (N_FOX, DEC_BATCH, PAST_LEN, FOX_HEADS, HEAD_DIM), F32),
        "cache_fox_v": nrm(ks[3], (N_FOX, DEC_BATCH, PAST_LEN, FOX_HEADS, HEAD_DIM), F32),
        "cache_fox_logf": jax.nn.log_sigmoid(3.0 + 0.5 * nrm(ks[4], (N_FOX, DEC_BATCH, PAST_LEN, FOX_HEADS), F32)),
        "cache_swa_k": nrm(ks[5], (N_SWA, DEC_BATCH, swa_keep, SWA_KV_HEADS, HEAD_DIM), F32),
        "cache_swa_v": nrm(ks[6], (N_SWA, DEC_BATCH, swa_keep, SWA_KV_HEADS, HEAD_DIM), F32),
        "c_prompt": nrm(ks[7], (BATCH, D_MODEL), F32),
        "c_sample": nrm(ks[8], (DEC_BATCH, D_MODEL), F32),
        "mod_w": nrm(ks[9], (DEPTH, D_MODEL, 6 * D_MODEL), F32) * (0.1 * d_in),
        "mod_b": 0.02 * nrm(ks[10], (DEPTH, 6 * D_MODEL), F32),
        "ln1_g": 1.0 + 0.02 * nrm(ks[11], (DEPTH, D_MODEL), F32),
        "ln1_b": 0.02 * nrm(ks[12], (DEPTH, D_MODEL), F32),
        "ln2_g": 1.0 + 0.02 * nrm(ks[13], (DEPTH, D_MODEL), F32),
        "ln2_b": 0.02 * nrm(ks[14], (DEPTH, D_MODEL), F32),
        "fox_w_in": nrm(ks[15], (N_FOX, D_MODEL, FOX_IN), F32) * d_in,
        "fox_b_in": jnp.concatenate([0.02 * nrm(ks[16], (N_FOX, 3 * FOX_HEADS * HEAD_DIM), F32),
                                     3.0 + 0.5 * nrm(ks[17], (N_FOX, FOX_HEADS), F32)], axis=-1),
        "fox_w_out": nrm(ks[18], (N_FOX, FOX_HEADS * HEAD_DIM, D_MODEL), F32) * ((FOX_HEADS * HEAD_DIM) ** -0.5 * DEEPNORM_BETA),
        "swa_w_in": nrm(ks[19], (N_SWA, D_MODEL, SWA_IN), F32) * d_in,
        "swa_b_in": 0.02 * nrm(ks[20], (N_SWA, SWA_IN), F32),
        "swa_sinks": 0.5 * nrm(ks[21], (N_SWA, SWA_HEADS), F32),
        "swa_w_out": nrm(ks[22], (N_SWA, SWA_HEADS * HEAD_DIM, D_MODEL), F32) * ((SWA_HEADS * HEAD_DIM) ** -0.5 * DEEPNORM_BETA),
        "router_g_w": nrm(ks[23], (DEPTH, D_MODEL, N_GROUPS), F32) * d_in,
        "router_g_b": 0.01 * nrm(ks[24], (DEPTH, N_GROUPS), F32),
        "router_e_w": nrm(ks[25], (DEPTH, N_GROUPS, D_MODEL, EXPERTS_PER_GROUP), F32) * d_in,
        "router_e_b": 0.01 * nrm(ks[26], (DEPTH, N_GROUPS, EXPERTS_PER_GROUP), F32),
        "w_gate": nrm(ks[27], (DEPTH, N_EXPERTS, D_MODEL, D_EXPERT), F32) * d_in,
        "w_up": nrm(ks[28], (DEPTH, N_EXPERTS, D_MODEL, D_EXPERT), F32) * d_in,
        "w_down": nrm(ks[29], (DEPTH, N_EXPERTS, D_EXPERT, D_MODEL), F32) * (D_EXPERT ** -0.5 * DEEPNORM_BETA),
    }


def reference(x_prompt, x_sample, cache_fox_k, cache_fox_v, cache_fox_logf, cache_swa_k, cache_swa_v,
              c_prompt, c_sample, mod_w, mod_b, ln1_g, ln1_b, ln2_g, ln2_b,
              fox_w_in, fox_b_in, fox_w_out, swa_w_in, swa_b_in, swa_sinks, swa_w_out,
              router_g_w, router_g_b, router_e_w, router_e_b, w_gate, w_up, w_down):
    past_len = cache_fox_k.shape[2]
    xp, xs = x_prompt, x_sample
    fkp, fvp, flp, skp, svp = [], [], [], [], []
    fks, fvs, fls, sks, svs = [], [], [], [], []
    for l in range(DEPTH):
        j = l // 2
        mp = modulation(c_prompt, mod_w[l], mod_b[l])
        ms = modulation(c_sample, mod_w[l], mod_b[l])
        hp = xp * (1.0 + mp[1]) + mp[0]
        hs = xs * (1.0 + ms[1]) + ms[0]
        if l % 2 == 0:
            ap, k_p, v_p, lf_p = fox_prompt(hp, fox_w_in[j], fox_b_in[j], fox_w_out[j])
            a_s, k_s, v_s, lf_s = fox_sample(hs, cache_fox_k[j], cache_fox_v[j], cache_fox_logf[j],
                                             fox_w_in[j], fox_b_in[j], fox_w_out[j])
            fkp.append(k_p); fvp.append(v_p); flp.append(lf_p)
            fks.append(k_s); fvs.append(v_s); fls.append(lf_s)
        else:
            ap, k_p, v_p = swa_prompt(hp, swa_w_in[j], swa_b_in[j], swa_sinks[j], swa_w_out[j])
            a_s, k_s, v_s = swa_sample(hs, cache_swa_k[j], cache_swa_v[j], past_len,
                                       swa_w_in[j], swa_b_in[j], swa_sinks[j], swa_w_out[j])
            skp.append(k_p); svp.append(v_p)
            sks.append(k_s); svs.append(v_s)
        xp = deepnorm_residual(xp, ap, mp[2], ln1_g[l], ln1_b[l])
        xs = deepnorm_residual(xs, a_s, ms[2], ln1_g[l], ln1_b[l])
        hp = xp * (1.0 + mp[4]) + mp[3]
        hs = xs * (1.0 + ms[4]) + ms[3]
        fp_ = hier_moe(hp, router_g_w[l], router_g_b[l], router_e_w[l], router_e_b[l], w_gate[l], w_up[l], w_down[l])
        fs_ = hier_moe(hs, router_g_w[l], router_g_b[l], router_e_w[l], router_e_b[l], w_gate[l], w_up[l], w_down[l])
        xp = deepnorm_residual(xp, fp_, mp[5], ln2_g[l], ln2_b[l])
        xs = deepnorm_residual(xs, fs_, ms[5], ln2_g[l], ln2_b[l])
    return (xp, xs,
            jnp.stack(fkp), jnp.stack(fvp), jnp.stack(flp), jnp.stack(skp), jnp.stack(svp),
            jnp.stack(fks), jnp.stack(fvs), jnp.stack(fls), jnp.stack(sks), jnp.stack(svs))
```

```python
import functools

import numpy as np
import jax
import jax.numpy as jnp
from jax import lax
from jax.experimental import pallas as pl
from jax.experimental.pallas import tpu as pltpu

F32 = jnp.float32
BF16 = jnp.bfloat16

HEAD_DIM = 64
CHUNK = 64
WINDOW = 128
SWA_GROUP = 4
N_GROUPS = 4
EXPERTS_PER_GROUP = 4
ROPE_THETA = 10000.0
LN_EPS = 1e-5
ATT_SCALE = HEAD_DIM ** -0.5
NEG = -1e30

LANES = 128
VMEM_LIMIT = 56 * 1024 * 1024
ROW_TILE = 512
FOX_TILE = 256
FOX_CACHE_TILE = 512
SWA_TILE = 256
AUX_PER_HEAD = 6
ROUTE_GROUP_LANE = 0
ROUTE_GATE_LANE = 4


def _dot(a, b):
    return jnp.dot(a, b, preferred_element_type=F32)


def _dot_nt(a, b):
    return lax.dot_general(a, b, (((1,), (1,)), ((), ())), preferred_element_type=F32)


def _split3(x):
    hi = x.astype(BF16)
    r = x - hi.astype(F32)
    mid = r.astype(BF16)
    lo = (r - mid.astype(F32)).astype(BF16)
    return hi, mid, lo


def _dot3(a3, b):
    return _dot(a3[0], b[0]) + _dot(a3[1], b[1]) + _dot(a3[2], b[2])


def _layer_norm(y, g, b):
    mu = jnp.mean(y, axis=-1, keepdims=True)
    d = y - mu
    var = jnp.mean(d * d, axis=-1, keepdims=True)
    return d * lax.rsqrt(var + LN_EPS) * g + b


def _params(*sem):
    return pltpu.CompilerParams(dimension_semantics=sem, vmem_limit_bytes=VMEM_LIMIT)


def _row_tiles(bsz, length):
    if length >= ROW_TILE:
        return 1, ROW_TILE
    return ROW_TILE // length, length


def _mod_kernel(c_ref, w_ref, b_ref, o_ref):
    c = c_ref[...]
    s = (c / (1.0 + jnp.exp(-c))).astype(BF16)
    o_ref[0] = _dot(s, w_ref[0].astype(BF16)) + b_ref[0]


def _modulation(c_all, mod_w, mod_b):
    depth, d, n = mod_w.shape
    rows = c_all.shape[0]
    tn = n // 4
    return pl.pallas_call(
        _mod_kernel,
        grid=(depth, n // tn),
        in_specs=[pl.BlockSpec((rows, d), lambda l, j: (0, 0)),
                  pl.BlockSpec((1, d, tn), lambda l, j: (l, 0, j)),
                  pl.BlockSpec((1, 1, tn), lambda l, j: (l, 0, j))],
        out_specs=pl.BlockSpec((1, rows, tn), lambda l, j: (l, 0, j)),
        out_shape=jax.ShapeDtypeStruct((depth, rows, n), F32),
        compiler_params=_params("arbitrary", "arbitrary"),
        name="modulation",
    )(c_all, mod_w, mod_b.reshape(depth, 1, n))


def _fox_aux_constants(n_heads):
    pq = np.zeros((3, LANES, LANES), np.float32)
    pk = np.zeros((3, LANES, LANES), np.float32)
    oq = np.zeros((1, LANES), np.float32)
    ok = np.zeros((1, LANES), np.float32)
    for h in range(n_heads):
        base = AUX_PER_HEAD * h
        for c in range(3):
            pq[c, h, base + c] = 1.0
            pk[c, h, base + 3 + c] = 1.0
            oq[0, base + 3 + c] = 1.0
            ok[0, base + c] = 1.0
    return pq, pk, oq, ok


def _inproj_fox_kernel(x_ref, sh_ref, sc_ref, wqkv_ref, bqkv_ref, wf_ref, bf_ref, tri_ref, pq_ref, pk_ref,
                       oq_ref, ok_ref, q_ref, kb_ref, vb_ref, k_ref, v_ref, lf_ref, qaux_ref, kaux_ref,
                       carry_ref, *, n_heads):
    bb, lt, d = x_ref.shape
    tm = bb * lt
    h = (x_ref[...] * (1.0 + sc_ref[...]) + sh_ref[...]).reshape(tm, d).astype(BF16)
    p = _dot(h, wqkv_ref[...]) + bqkv_ref[...]
    q, k, v = p[:, :d], p[:, d:2 * d], p[:, 2 * d:]
    q_ref[...] = (q * ATT_SCALE).astype(BF16).reshape(bb, lt, d)
    k_ref[...] = k.reshape(bb, lt, d)
    v_ref[...] = v.reshape(bb, lt, d)
    kb_ref[...] = k.astype(BF16).reshape(bb, lt, d)
    vb_ref[...] = v.astype(BF16).reshape(bb, lt, d)

    pf = _dot(h, wf_ref[...]) + bf_ref[...]
    lane = lax.broadcasted_iota(jnp.int32, pf.shape, 1)
    lf = jnp.where(lane < n_heads, jnp.minimum(pf, 0.0) - jnp.log1p(jnp.exp(-jnp.abs(pf))), 0.0)
    lf_ref[...] = lf[:, :n_heads].reshape(bb, lt, n_heads)

    @pl.when(pl.program_id(1) == 0)
    def _():
        carry_ref[...] = jnp.zeros_like(carry_ref)

    tri = tri_ref[...]
    l3 = _split3(lf)
    cum = _dot(tri, l3[0]) + _dot(tri, l3[1]) + _dot(tri, l3[2]) + carry_ref[...]
    carry_ref[...] = cum[tm - 1:tm, :]
    c3 = _split3(cum)
    qaux_ref[...] = (_dot3(c3, pq_ref) + oq_ref[...]).astype(BF16).reshape(bb, lt, LANES)
    kaux_ref[...] = (ok_ref[...] - _dot3(c3, pk_ref)).astype(BF16).reshape(bb, lt, LANES)


def _inproj_fox(x, sh, sc, wqkv, bqkv, wf, bf_, n_heads):
    bsz, length, d = x.shape
    bb, lt = _row_tiles(bsz, length)
    tm = bb * lt
    r = np.arange(tm)
    tri = ((r[None, :] <= r[:, None]) & (r[None, :] // lt == r[:, None] // lt)).astype(np.float32)
    pq, pk, oq, ok = _fox_aux_constants(n_heads)
    const = lambda shape: pl.BlockSpec(shape, lambda b, i: (0,) * len(shape))
    row = lambda w: pl.BlockSpec((bb, lt, w), lambda b, i: (b, i, 0))
    mod = pl.BlockSpec((bb, 1, d), lambda b, i: (b, 0, 0))
    shp = lambda w, dt: jax.ShapeDtypeStruct((bsz, length, w), dt)
    return pl.pallas_call(
        functools.partial(_inproj_fox_kernel, n_heads=n_heads),
        grid=(bsz // bb, length // lt),
        in_specs=[row(d), mod, mod, const((d, 3 * d)), const((1, 3 * d)), const((d, LANES)), const((1, LANES)),
                  const((tm, tm)), const((3, LANES, LANES)), const((3, LANES, LANES)),
                  const((1, LANES)), const((1, LANES))],
        out_specs=[row(d), row(d), row(d), row(d), row(d), row(n_heads), row(LANES), row(LANES)],
        out_shape=[shp(d, BF16), shp(d, BF16), shp(d, BF16), shp(d, F32), shp(d, F32), shp(n_heads, F32),
                   shp(LANES, BF16), shp(LANES, BF16)],
        scratch_shapes=[pltpu.VMEM((1, LANES), F32)],
        compiler_params=_params("arbitrary", "arbitrary"),
        name="inproj_fox",
    )(x, sh, sc, wqkv, bqkv, wf, bf_, jnp.asarray(tri, BF16), jnp.asarray(pq, BF16), jnp.asarray(pk, BF16),
      jnp.asarray(oq), jnp.asarray(ok))


def _stacked_queries(q, qaux, pair):
    qf = q.astype(F32)
    af = qaux.astype(F32)
    lane = lax.broadcasted_iota(jnp.int32, qf.shape, 1)
    base = 2 * AUX_PER_HEAD * pair
    first = jnp.concatenate([jnp.where(lane < HEAD_DIM, qf, 0.0),
                             jnp.where((lane >= base) & (lane < base + AUX_PER_HEAD), af, 0.0)], axis=1)
    second = jnp.concatenate([jnp.where(lane >= HEAD_DIM, qf, 0.0),
                              jnp.where((lane >= base + AUX_PER_HEAD) & (lane < base + 2 * AUX_PER_HEAD), af, 0.0)],
                             axis=1)
    return jnp.concatenate([first, second], axis=0).astype(BF16)


def _softmax_update(s, vt, m, l, acc):
    m_new = jnp.maximum(m, jnp.max(s, axis=1, keepdims=True))
    alpha = jnp.exp(m - m_new)
    pe = jnp.exp(s - m_new)
    l_new = alpha * l + jnp.sum(pe, axis=1, keepdims=True)
    acc_new = alpha * acc + _dot(pe.astype(BF16), vt)
    return m_new, l_new, acc_new


def _merge_pair(acc, l, n):
    out = acc / l
    lane = lax.broadcasted_iota(jnp.int32, (n, LANES), 1)
    return jnp.where(lane < HEAD_DIM, out[:n], out[n:])


def _causal_mask(n_rows, n_cols, period):
    row = lax.broadcasted_iota(jnp.int32, (n_rows, n_cols), 0)
    col = lax.broadcasted_iota(jnp.int32, (n_rows, n_cols), 1)
    row = jnp.where(row >= period, row - period, row)
    return col <= row


def _fox_attn_kernel(q_ref, qaux_ref, k_ref, kaux_ref, v_ref, o_ref):
    t = q_ref.shape[1]
    pair = pl.program_id(1)
    i = pl.program_id(2)
    qs = _stacked_queries(q_ref[0], qaux_ref[0], pair)

    def tile(j, carry, mask):
        start = pl.multiple_of(j * t, t)
        kt = jnp.concatenate([k_ref[0, pl.ds(start, t), :], kaux_ref[0, pl.ds(start, t), :]], axis=1)
        s = _dot_nt(qs, kt)
        if mask is not None:
            s = jnp.where(mask, s, NEG)
        return _softmax_update(s, v_ref[0, pl.ds(start, t), :], *carry)

    init = (jnp.full((2 * t, 1), NEG, F32), jnp.zeros((2 * t, 1), F32), jnp.zeros((2 * t, LANES), F32))
    carry = lax.fori_loop(0, i, lambda j, c: tile(j, c, None), init)
    _, l, acc = tile(i, carry, _causal_mask(2 * t, t, t))
    o_ref[0] = _merge_pair(acc, l, t).astype(BF16)


def _fox_attention(q, qaux, kb, kaux, vb):
    bsz, s, d = q.shape
    t = FOX_TILE
    pairs = d // LANES
    return pl.pallas_call(
        _fox_attn_kernel,
        grid=(bsz, pairs, s // t),
        in_specs=[pl.BlockSpec((1, t, LANES), lambda b, p, i: (b, i, p)),
                  pl.BlockSpec((1, t, LANES), lambda b, p, i: (b, i, 0)),
                  pl.BlockSpec((1, s, LANES), lambda b, p, i: (b, 0, p)),
                  pl.BlockSpec((1, s, LANES), lambda b, p, i: (b, 0, 0)),
                  pl.BlockSpec((1, s, LANES), lambda b, p, i: (b, 0, p))],
        out_specs=pl.BlockSpec((1, t, LANES), lambda b, p, i: (b, i, p)),
        out_shape=jax.ShapeDtypeStruct((bsz, s, d), BF16),
        compiler_params=_params("arbitrary", "arbitrary", "arbitrary"),
        name="fox_attention",
    )(q, qaux, kb, kaux, vb)


def _cache_aux_kernel(lf_ref, upper_ref, pk_ref, ok_ref, kaux_ref, carry_ref):
    @pl.when(pl.program_id(1) == 0)
    def _():
        carry_ref[...] = jnp.zeros_like(carry_ref)

    lf = lf_ref[0]
    upper = upper_ref[...]
    l3 = _split3(lf)
    carry = carry_ref[...]
    suffix = _dot(upper, l3[0]) + _dot(upper, l3[1]) + _dot(upper, l3[2]) + carry
    carry_ref[...] = carry + jnp.sum(lf, axis=0, keepdims=True)
    kaux_ref[0] = (_dot3(_split3(suffix), pk_ref) + ok_ref[...]).astype(BF16)


def _cache_aux(clogf_padded, n_heads):
    bsz, plen, _ = clogf_padded.shape
    t = FOX_CACHE_TILE
    nt = plen // t
    r = np.arange(t)
    upper = (r[None, :] > r[:, None]).astype(np.float32)
    _, pk, _, ok = _fox_aux_constants(n_heads)
    const = lambda shape: pl.BlockSpec(shape, lambda b, j: (0,) * len(shape))
    return pl.pallas_call(
        _cache_aux_kernel,
        grid=(bsz, nt),
        in_specs=[pl.BlockSpec((1, t, LANES), lambda b, j: (b, nt - 1 - j, 0)),
                  const((t, t)), const((3, LANES, LANES)), const((1, LANES))],
        out_specs=pl.BlockSpec((1, t, LANES), lambda b, j: (b, nt - 1 - j, 0)),
        out_shape=jax.ShapeDtypeStruct((bsz, plen, LANES), BF16),
        scratch_shapes=[pltpu.VMEM((1, LANES), F32)],
        compiler_params=_params("arbitrary", "arbitrary"),
        name="fox_cache_aux",
    )(clogf_padded, jnp.asarray(upper, BF16), jnp.asarray(pk, BF16), jnp.asarray(ok))


def _fox_sample_kernel(q_ref, qaux_ref, ck_ref, cv_ref, kauxc_ref, kn_ref, vn_ref, kauxn_ref, o_ref,
                       qs_ref, m_ref, l_ref, acc_ref):
    step = pl.program_id(1)
    n_new = q_ref.shape[1]
    pairs = q_ref.shape[2] // LANES

    @pl.when(step == 0)
    def _():
        for p in range(pairs):
            qs_ref[p] = _stacked_queries(q_ref[0, :, p * LANES:(p + 1) * LANES], qaux_ref[0], p)
        m_ref[...] = jnp.full(m_ref.shape, NEG, F32)
        l_ref[...] = jnp.zeros(l_ref.shape, F32)
        acc_ref[...] = jnp.zeros(acc_ref.shape, F32)

    def update(p, kt, vt, mask):
        s = _dot_nt(qs_ref[p], kt)
        if mask is not None:
            s = jnp.where(mask, s, NEG)
        m_ref[p], l_ref[p], acc_ref[p] = _softmax_update(s, vt, m_ref[p], l_ref[p], acc_ref[p])

    kauxc = kauxc_ref[0]
    for p in range(pairs):
        sl = slice(p * LANES, (p + 1) * LANES)
        update(p, jnp.concatenate([ck_ref[:, sl].astype(BF16), kauxc], axis=1), cv_ref[:, sl].astype(BF16), None)

    @pl.when(step == pl.num_programs(1) - 1)
    def _():
        mask = _causal_mask(2 * n_new, n_new, n_new)
        kauxn = kauxn_ref[0]
        for p in range(pairs):
            sl = slice(p * LANES, (p + 1) * LANES)
            update(p, jnp.concatenate([kn_ref[0, :, sl], kauxn], axis=1), vn_ref[0, :, sl], mask)
            o_ref[0, :, sl] = _merge_pair(acc_ref[p], l_ref[p], n_new).astype(BF16)


def _fox_sample_attention(q, qaux, cache_k, cache_v, layer, kauxc, kn, vn, kauxn):
    bsz, n_new, d = q.shape
    plen = cache_k.shape[2]
    t = FOX_CACHE_TILE
    pairs = d // LANES
    new = lambda w: pl.BlockSpec((1, n_new, w), lambda b, j: (b, 0, 0))
    cache = pl.BlockSpec((None, None, t, d), lambda b, j: (layer, b, j, 0))
    return pl.pallas_call(
        _fox_sample_kernel,
        grid=(bsz, plen // t),
        in_specs=[new(d), new(LANES), cache, cache, pl.BlockSpec((1, t, LANES), lambda b, j: (b, j, 0)),
                  new(d), new(d), new(LANES)],
        out_specs=new(d),
        out_shape=jax.ShapeDtypeStruct((bsz, n_new, d), BF16),
        scratch_shapes=[pltpu.VMEM((pairs, 2 * n_new, 2 * LANES), BF16),
                        pltpu.VMEM((pairs, 2 * n_new, 1), F32),
                        pltpu.VMEM((pairs, 2 * n_new, 1), F32),
                        pltpu.VMEM((pairs, 2 * n_new, LANES), F32)],
        compiler_params=_params("arbitrary", "arbitrary"),
        name="fox_sample_attention",
    )(q, qaux, cache_k, cache_v, kauxc, kn, vn, kauxn)


def _rope_tables(positions):
    half = HEAD_DIM // 2
    inv = ROPE_THETA ** (-jnp.arange(half, dtype=F32) / half)
    ang = positions.astype(F32)[:, None] * inv[None, :]
    cos, sin = jnp.cos(ang), jnp.sin(ang)
    reps = LANES // HEAD_DIM
    return jnp.tile(jnp.concatenate([cos, cos], axis=1), (1, reps)), jnp.tile(jnp.concatenate([-sin, sin], axis=1), (1, reps))


def _rope(x, cos, sin_signed):
    n = x.shape[1]
    reps = n // LANES
    lane = lax.broadcasted_iota(jnp.int32, x.shape, 1)
    half = HEAD_DIM // 2
    partner = jnp.where((lane & (HEAD_DIM - 1)) < half, pltpu.roll(x, n - half, 1), pltpu.roll(x, half, 1))
    return x * jnp.concatenate([cos] * reps, axis=1) + partner * jnp.concatenate([sin_signed] * reps, axis=1)


def _inproj_swa_kernel(x_ref, sh_ref, sc_ref, w_ref, b_ref, cos_ref, sin_ref, q_ref, kb_ref, vb_ref, k_ref, v_ref):
    bb, lt, d = x_ref.shape
    tm = bb * lt
    nq = q_ref.shape[1] * HEAD_DIM
    nkv = kb_ref.shape[1] * HEAD_DIM
    h = (x_ref[...] * (1.0 + sc_ref[...]) + sh_ref[...]).reshape(tm, d).astype(BF16)
    p = _dot(h, w_ref[...]) + b_ref[...]
    cos = jnp.concatenate([cos_ref[...]] * bb, axis=0)
    sin = jnp.concatenate([sin_ref[...]] * bb, axis=0)
    q = _rope(p[:, :nq], cos, sin)
    k = _rope(p[:, nq:nq + nkv], cos, sin)
    v = p[:, nq + nkv:]
    k_ref[...] = k.reshape(bb, lt, nkv)
    v_ref[...] = v.reshape(bb, lt, nkv)
    qb = (q * ATT_SCALE).astype(BF16)
    kb = k.astype(BF16)
    vb = v.astype(BF16)
    for hd in range(nq // HEAD_DIM):
        q_ref[:, hd] = qb[:, hd * HEAD_DIM:(hd + 1) * HEAD_DIM].reshape(bb, lt, HEAD_DIM)
    for hd in range(nkv // HEAD_DIM):
        kb_ref[:, hd] = kb[:, hd * HEAD_DIM:(hd + 1) * HEAD_DIM].reshape(bb, lt, HEAD_DIM)
        vb_ref[:, hd] = vb[:, hd * HEAD_DIM:(hd + 1) * HEAD_DIM].reshape(bb, lt, HEAD_DIM)


def _inproj_swa(x, sh, sc, w, b, positions, n_q_heads):
    bsz, length, d = x.shape
    n = w.shape[1]
    n_kv_heads = n_q_heads // SWA_GROUP
    nkv = n_kv_heads * HEAD_DIM
    bb, lt = _row_tiles(bsz, length)
    cos, sin = _rope_tables(positions)
    const = lambda shape: pl.BlockSpec(shape, lambda bi, i: (0,) * len(shape))
    mod = pl.BlockSpec((bb, 1, d), lambda bi, i: (bi, 0, 0))
    tab = pl.BlockSpec((lt, LANES), lambda bi, i: (i, 0))
    heads = lambda nh: pl.BlockSpec((bb, nh, lt, HEAD_DIM), lambda bi, i: (bi, 0, i, 0))
    rows = lambda wd: pl.BlockSpec((bb, lt, wd), lambda bi, i: (bi, i, 0))
    return pl.pallas_call(
        _inproj_swa_kernel,
        grid=(bsz // bb, length // lt),
        in_specs=[rows(d), mod, mod, const((d, n)), const((1, n)), tab, tab],
        out_specs=[heads(n_q_heads), heads(n_kv_heads), heads(n_kv_heads), rows(nkv), rows(nkv)],
        out_shape=[jax.ShapeDtypeStruct((bsz, n_q_heads, length, HEAD_DIM), BF16),
                   jax.ShapeDtypeStruct((bsz, n_kv_heads, length, HEAD_DIM), BF16),
                   jax.ShapeDtypeStruct((bsz, n_kv_heads, length, HEAD_DIM), BF16),
                   jax.ShapeDtypeStruct((bsz, length, nkv), F32),
                   jax.ShapeDtypeStruct((bsz, length, nkv), F32)],
        compiler_params=_params("arbitrary", "arbitrary"),
        name="inproj_swa",
    )(x, sh, sc, w, b, cos, sin)


def _swa_attn_kernel(sinks_ref, q_ref, k_ref, v_ref, o_ref, *, q_offset):
    g, tq, _ = q_ref.shape[1:]
    wn = tq + WINDOW
    kvh = pl.program_id(1)
    q0 = q_offset + pl.program_id(2) * tq
    ws = pl.multiple_of(jnp.maximum(q0 - WINDOW, 0), CHUNK)
    kw = k_ref[0, 0, pl.ds(ws, wn), :]
    vw = v_ref[0, 0, pl.ds(ws, wn), :]
    shift = CHUNK.bit_length() - 1
    qc = jnp.right_shift(q0 + lax.broadcasted_iota(jnp.int32, (tq, wn), 0), shift)
    kc = jnp.right_shift(ws + lax.broadcasted_iota(jnp.int32, (tq, wn), 1), shift)
    visible = (kc <= qc) & (kc >= qc - WINDOW // CHUNK)
    for i in range(g):
        sink = sinks_ref[kvh * g + i]
        s = jnp.where(visible, _dot_nt(q_ref[0, i], kw), NEG)
        m = jnp.maximum(jnp.max(s, axis=1, keepdims=True), sink)
        pe = jnp.exp(s - m)
        denom = jnp.sum(pe, axis=1, keepdims=True) + jnp.exp(sink - m)
        o = _dot(pe.astype(BF16), vw) / denom
        o_ref[0, :, i * HEAD_DIM:(i + 1) * HEAD_DIM] = o.astype(BF16)


def _swa_attention(q, k, v, sinks, q_offset):
    bsz, nh, lq, _ = q.shape
    nkv, lk = k.shape[1], k.shape[2]
    g = nh // nkv
    tq = min(SWA_TILE, lq)
    kv = pl.BlockSpec((1, 1, lk, HEAD_DIM), lambda b, h, i, s: (b, h, 0, 0))
    return pl.pallas_call(
        functools.partial(_swa_attn_kernel, q_offset=q_offset),
        grid_spec=pltpu.PrefetchScalarGridSpec(
            num_scalar_prefetch=1,
            grid=(bsz, nkv, lq // tq),
            in_specs=[pl.BlockSpec((1, g, tq, HEAD_DIM), lambda b, h, i, s: (b, h, i, 0)), kv, kv],
            out_specs=pl.BlockSpec((1, tq, g * HEAD_DIM), lambda b, h, i, s: (b, i, h))),
        out_shape=jax.ShapeDtypeStruct((bsz, lq, nh * HEAD_DIM), BF16),
        compiler_params=_params("arbitrary", "arbitrary", "arbitrary"),
        name="swa_attention",
    )(sinks, q, k, v)


def _route(r):
    lane = lax.broadcasted_iota(jnp.int32, r.shape, 1).astype(F32)
    big = float(LANES)
    gl = jnp.where(lane < N_GROUPS, r, NEG)
    gmax = jnp.max(gl, axis=1, keepdims=True)
    gp = 1.0 / jnp.sum(jnp.exp(gl - gmax), axis=1, keepdims=True)
    gi = jnp.min(jnp.where(gl == gmax, lane, big), axis=1, keepdims=True)
    lo = ROUTE_GATE_LANE + EXPERTS_PER_GROUP * gi
    el = jnp.where((lane >= lo) & (lane < lo + EXPERTS_PER_GROUP), r, NEG)
    e1 = jnp.max(el, axis=1, keepdims=True)
    esum = jnp.sum(jnp.exp(el - e1), axis=1, keepdims=True)
    i1 = jnp.min(jnp.where(el == e1, lane, big), axis=1, keepdims=True)
    el2 = jnp.where(lane == i1, NEG, el)
    e2 = jnp.max(el2, axis=1, keepdims=True)
    i2 = jnp.min(jnp.where(el2 == e2, lane, big), axis=1, keepdims=True)
    p1 = 1.0 / esum
    p2 = jnp.exp(e2 - e1) / esum
    w1 = gp * p1 / (p1 + p2)
    w2 = gp * p2 / (p1 + p2)
    gates = jnp.where(lane == i1, w1, jnp.where(lane == i2, w2, 0.0))
    return jnp.where(lane == ROUTE_GROUP_LANE, gi, gates)


def _outproj_kernel(x_ref, o_ref, w_ref, g1_ref, lng_ref, lnb_ref, sh_ref, sc_ref, wrh_ref, wrl_ref, br_ref,
                    x1_ref, h2_ref, route_ref, *, alpha):
    bb, lt, d = x_ref.shape
    tm = bb * lt
    a = _dot(o_ref[...].reshape(tm, d), w_ref[...]).reshape(bb, lt, d)
    x1 = _layer_norm(alpha * x_ref[...] + (1.0 + g1_ref[...]) * a, lng_ref[...], lnb_ref[...])
    x1_ref[...] = x1
    h2 = (x1 * (1.0 + sc_ref[...]) + sh_ref[...]).reshape(tm, d)
    hi = h2.astype(BF16)
    lo = (h2 - hi.astype(F32)).astype(BF16)
    h2_ref[...] = hi.reshape(bb, lt, d)
    r = _dot(hi, wrh_ref[...]) + _dot(hi, wrl_ref[...]) + _dot(lo, wrh_ref[...]) + br_ref[...]
    route_ref[...] = _route(r).reshape(bb, lt, LANES)


def _outproj(x, o, w_out, g1, lng, lnb, sh2, sc2, wr_hi, wr_lo, br, alpha):
    bsz, length, d = x.shape
    bb, lt = _row_tiles(bsz, length)
    const = lambda shape: pl.BlockSpec(shape, lambda b, i: (0,) * len(shape))
    rows = lambda w: pl.BlockSpec((bb, lt, w), lambda b, i: (b, i, 0))
    mod = pl.BlockSpec((bb, 1, d), lambda b, i: (b, 0, 0))
    return pl.pallas_call(
        functools.partial(_outproj_kernel, alpha=alpha),
        grid=(bsz // bb, length // lt),
        in_specs=[rows(d), rows(d), const((d, d)), mod, const((1, d)), const((1, d)), mod, mod,
                  const((d, LANES)), const((d, LANES)), const((1, LANES))],
        out_specs=[rows(d), rows(d), rows(LANES)],
        out_shape=[jax.ShapeDtypeStruct((bsz, length, d), F32), jax.ShapeDtypeStruct((bsz, length, d), BF16),
                   jax.ShapeDtypeStruct((bsz, length, LANES), F32)],
        compiler_params=_params("arbitrary", "arbitrary"),
        name="outproj_router",
    )(x, o, w_out, g1, lng, lnb, sh2, sc2, wr_hi, wr_lo, br)


def _moe_kernel(h_ref, route_ref, wg_ref, wu_ref, wd_ref, x1_ref, g2_ref, lng_ref, lnb_ref, x2_ref, acc_ref,
                *, alpha):
    bb, lt, d = h_ref.shape
    tm = bb * lt
    grp = pl.program_id(2)

    @pl.when(grp == 0)
    def _():
        acc_ref[...] = jnp.zeros_like(acc_ref)

    h = h_ref[...].reshape(tm, d)
    route = route_ref[...].reshape(tm, LANES)
    lane = lax.broadcasted_iota(jnp.int32, route.shape, 1)
    base = ROUTE_GATE_LANE + EXPERTS_PER_GROUP * grp
    acts = []
    for e in range(EXPERTS_PER_GROUP):
        gate = jnp.sum(jnp.where(lane == base + e, route, 0.0), axis=1, keepdims=True)
        hg = _dot(h, wg_ref[e])
        hu = _dot(h, wu_ref[e])
        acts.append(((hg / (1.0 + jnp.exp(-hg))) * hu * gate).astype(BF16))
    a = jnp.concatenate(acts, axis=1)
    acc_ref[...] += _dot(a, wd_ref[...].reshape(a.shape[1], d))

    @pl.when(grp == pl.num_programs(2) - 1)
    def _():
        y = alpha * x1_ref[...] + (1.0 + g2_ref[...]) * acc_ref[...].reshape(bb, lt, d)
        x2_ref[...] = _layer_norm(y, lng_ref[...], lnb_ref[...])


def _moe(h2, route, wg, wu, wd, layer, x1, g2, lng, lnb, alpha):
    bsz, length, d = h2.shape
    f = wg.shape[-1]
    bb, lt = _row_tiles(bsz, length)
    const = lambda shape: pl.BlockSpec(shape, lambda b, i, g: (0,) * len(shape))
    rows = lambda w: pl.BlockSpec((bb, lt, w), lambda b, i, g: (b, i, 0))
    mod = pl.BlockSpec((bb, 1, d), lambda b, i, g: (b, 0, 0))
    epg = EXPERTS_PER_GROUP
    return pl.pallas_call(
        functools.partial(_moe_kernel, alpha=alpha),
        grid=(bsz // bb, length // lt, N_GROUPS),
        in_specs=[rows(d), rows(LANES),
                  pl.BlockSpec((None, epg, d, f), lambda b, i, g: (layer, g, 0, 0)),
                  pl.BlockSpec((None, epg, d, f), lambda b, i, g: (layer, g, 0, 0)),
                  pl.BlockSpec((None, epg, f, d), lambda b, i, g: (layer, g, 0, 0)),
                  rows(d), mod, const((1, d)), const((1, d))],
        out_specs=rows(d),
        out_shape=jax.ShapeDtypeStruct((bsz, length, d), F32),
        scratch_shapes=[pltpu.VMEM((bb * lt, d), F32)],
        compiler_params=_params("arbitrary", "arbitrary", "arbitrary"),
        name="moe_deepnorm",
    )(h2, route, wg, wu, wd, x1, g2, lng, lnb)


def kernel(x_prompt, x_sample, cache_fox_k, cache_fox_v, cache_fox_logf, cache_swa_k, cache_swa_v, c_prompt, c_sample, mod_w, mod_b, ln1_g, ln1_b, ln2_g, ln2_b, fox_w_in, fox_b_in, fox_w_out, swa_w_in, swa_b_in, swa_sinks, swa_w_out, router_g_w, router_g_b, router_e_w, router_e_b, w_gate, w_up, w_down):
    depth = mod_w.shape[0]
    bp, seq, d = x_prompt.shape
    bs, dec = x_sample.shape[:2]
    past = cache_fox_k.shape[2]
    fox_heads = cache_fox_k.shape[3]
    swa_kv = cache_swa_k.shape[3]
    swa_heads = swa_kv * SWA_GROUP
    keep = cache_swa_k.shape[2]
    alpha = (2.0 * depth) ** 0.25

    n_c = bp + bs
    c_all = jnp.pad(jnp.concatenate([c_prompt, c_sample], axis=0), ((0, -n_c % 16), (0, 0)))
    mods = _modulation(c_all, mod_w, mod_b)

    def mod_parts(l, lo, hi):
        return [mods[l, lo:hi, i * d:(i + 1) * d].reshape(hi - lo, 1, d) for i in range(6)]

    ck = cache_fox_k.reshape(cache_fox_k.shape[:3] + (d,))
    cv = cache_fox_v.reshape(cache_fox_v.shape[:3] + (d,))
    clogf = jnp.pad(cache_fox_logf, ((0, 0), (0, 0), (0, 0), (0, LANES - fox_heads)))
    wg_b, wu_b, wd_b = w_gate.astype(BF16), w_up.astype(BF16), w_down.astype(BF16)
    n_grp = router_g_w.shape[-1]
    n_exp = w_gate.shape[1]

    xp, xs = x_prompt, x_sample
    fkp, fvp, flp, skp, svp = [], [], [], [], []
    fks, fvs, fls, sks, svs = [], [], [], [], []
    for l in range(depth):
        j = l // 2
        mp = mod_parts(l, 0, bp)
        ms = mod_parts(l, bp, n_c)
        if l % 2 == 0:
            n_qkv = 3 * d
            wqkv = fox_w_in[j, :, :n_qkv].astype(BF16)
            bqkv = fox_b_in[j, :n_qkv].reshape(1, n_qkv)
            wf = jnp.pad(fox_w_in[j, :, n_qkv:], ((0, 0), (0, LANES - fox_heads))).astype(BF16)
            bf_ = jnp.pad(fox_b_in[j, n_qkv:], (0, LANES - fox_heads)).reshape(1, LANES)
            q, kb, vb, k_p, v_p, lf_p, qaux, kaux = _inproj_fox(xp, mp[0], mp[1], wqkv, bqkv, wf, bf_, fox_heads)
            op = _fox_attention(q, qaux, kb, kaux, vb)
            q, kb, vb, k_s, v_s, lf_s, qaux, kaux = _inproj_fox(xs, ms[0], ms[1], wqkv, bqkv, wf, bf_, fox_heads)
            kauxc = _cache_aux(clogf[j], fox_heads)
            os_ = _fox_sample_attention(q, qaux, ck, cv, j, kauxc, kb, vb, kaux)
            hshape = (fox_heads, HEAD_DIM)
            fkp.append(k_p.reshape(bp, seq, *hshape)); fvp.append(v_p.reshape(bp, seq, *hshape)); flp.append(lf_p)
            fks.append(k_s.reshape(bs, dec, *hshape)); fvs.append(v_s.reshape(bs, dec, *hshape)); fls.append(lf_s)
            w_out = fox_w_out[j].astype(BF16)
        else:
            w_in = swa_w_in[j].astype(BF16)
            b_in = swa_b_in[j].reshape(1, -1)
            q, kb, vb, k_p, v_p = _inproj_swa(xp, mp[0], mp[1], w_in, b_in, jnp.arange(seq), swa_heads)
            op = _swa_attention(q, kb, vb, swa_sinks[j], 0)
            q, kb, vb, k_s, v_s = _inproj_swa(xs, ms[0], ms[1], w_in, b_in, past + jnp.arange(dec), swa_heads)
            hshape = (swa_kv, HEAD_DIM)
            k_all = jnp.concatenate([cache_swa_k[j], k_s.reshape(bs, dec, *hshape)], axis=1)
            v_all = jnp.concatenate([cache_swa_v[j], v_s.reshape(bs, dec, *hshape)], axis=1)
            os_ = _swa_attention(q, jnp.swapaxes(k_all, 1, 2).astype(BF16), jnp.swapaxes(v_all, 1, 2).astype(BF16),
                                 swa_sinks[j], keep)
            skp.append(k_p[:, seq - WINDOW:].reshape(bp, WINDOW, *hshape))
            svp.append(v_p[:, seq - WINDOW:].reshape(bp, WINDOW, *hshape))
            sks.append(k_all[:, k_all.shape[1] - keep:]); svs.append(v_all[:, v_all.shape[1] - keep:])
            w_out = swa_w_out[j].astype(BF16)

        wr = jnp.concatenate([router_g_w[l], jnp.moveaxis(router_e_w[l], 0, 1).reshape(d, n_exp)], axis=1)
        wr = jnp.pad(wr, ((0, 0), (0, LANES - n_grp - n_exp)))
        wr_hi = wr.astype(BF16)
        wr_lo = (wr - wr_hi.astype(F32)).astype(BF16)
        br = jnp.pad(jnp.concatenate([router_g_b[l], router_e_b[l].reshape(-1)]), (0, LANES - n_grp - n_exp))
        br = br.reshape(1, LANES)
        lng1, lnb1 = ln1_g[l].reshape(1, d), ln1_b[l].reshape(1, d)
        lng2, lnb2 = ln2_g[l].reshape(1, d), ln2_b[l].reshape(1, d)
        for grp_in in ("prompt", "sample"):
            x, o, m = (xp, op, mp) if grp_in == "prompt" else (xs, os_, ms)
            x1, h2, route = _outproj(x, o, w_out, m[2], lng1, lnb1, m[3], m[4], wr_hi, wr_lo, br, alpha)
            x2 = _moe(h2, route, wg_b, wu_b, wd_b, l, x1, m[5], lng2, lnb2, alpha)
            if grp_in == "prompt":
                xp = x2
            else:
                xs = x2
    return (xp, xs,
            jnp.stack(fkp), jnp.stack(fvp), jnp.stack(flp), jnp.stack(skp), jnp.stack(svp),
            jnp.stack(fks), jnp.stack(fvs), jnp.stack(fls), jnp.stack(sks), jnp.stack(svs))
```

```python
import functools

import numpy as np
import jax
import jax.numpy as jnp
from jax import lax
from jax.experimental import pallas as pl
from jax.experimental.pallas import tpu as pltpu

F32 = jnp.float32
BF16 = jnp.bfloat16

HEAD_DIM = 64
CHUNK = 64
WINDOW = 128
SWA_GROUP = 4
N_GROUPS = 4
EXPERTS_PER_GROUP = 4
ROPE_THETA = 10000.0
LN_EPS = 1e-5
ATT_SCALE = HEAD_DIM ** -0.5
NEG = -1e30
LOG2E = 1.4426950408889634

LANES = 128
VMEM_LIMIT = 56 * 1024 * 1024
ROW_TILE = 512
FOX_TILE = 256
FOX_CACHE_TILE = 1024
FOX_SUFFIX_TILE = 512
SWA_TILE = 256
AUX_PER_HEAD = 6
ROUTE_GROUP_LANE = 0
ROUTE_GATE_LANE = 4


def _dot(a, b):
    return jnp.dot(a, b, preferred_element_type=F32)


def _dot_nt(a, b):
    return lax.dot_general(a, b, (((1,), (1,)), ((), ())), preferred_element_type=F32)


def _split3(x):
    hi = x.astype(BF16)
    r = x - hi.astype(F32)
    mid = r.astype(BF16)
    lo = (r - mid.astype(F32)).astype(BF16)
    return hi, mid, lo


def _dot3(a3, b):
    return _dot(a3[0], b[0]) + _dot(a3[1], b[1]) + _dot(a3[2], b[2])


def _layer_norm(y, g, b):
    mu = jnp.mean(y, axis=-1, keepdims=True)
    d = y - mu
    var = jnp.mean(d * d, axis=-1, keepdims=True)
    return d * lax.rsqrt(var + LN_EPS) * g + b


def _params(*sem):
    return pltpu.CompilerParams(dimension_semantics=sem, vmem_limit_bytes=VMEM_LIMIT)


def _row_tiles(bsz, length):
    if length >= ROW_TILE:
        return 1, ROW_TILE
    return ROW_TILE // length, length


def _mod_kernel(c_ref, w_ref, b_ref, o_ref):
    c = c_ref[...]
    s = (c / (1.0 + jnp.exp(-c))).astype(BF16)
    o_ref[0] = _dot(s, w_ref[0].astype(BF16)) + b_ref[0]


def _modulation(c_all, mod_w, mod_b):
    depth, d, n = mod_w.shape
    rows = c_all.shape[0]
    tn = n // 4
    return pl.pallas_call(
        _mod_kernel,
        grid=(depth, n // tn),
        in_specs=[pl.BlockSpec((rows, d), lambda l, j: (0, 0)),
                  pl.BlockSpec((1, d, tn), lambda l, j: (l, 0, j)),
                  pl.BlockSpec((1, 1, tn), lambda l, j: (l, 0, j))],
        out_specs=pl.BlockSpec((1, rows, tn), lambda l, j: (l, 0, j)),
        out_shape=jax.ShapeDtypeStruct((depth, rows, n), F32),
        compiler_params=_params("arbitrary", "arbitrary"),
        name="modulation",
    )(c_all, mod_w, mod_b.reshape(depth, 1, n))


def _fox_aux_constants(n_heads):
    pq = np.zeros((3, LANES, LANES), np.float32)
    pk = np.zeros((3, LANES, LANES), np.float32)
    oq = np.zeros((1, LANES), np.float32)
    ok = np.zeros((1, LANES), np.float32)
    for h in range(n_heads):
        base = AUX_PER_HEAD * h
        for c in range(3):
            pq[c, h, base + c] = 1.0
            pk[c, h, base + 3 + c] = 1.0
            oq[0, base + 3 + c] = 1.0
            ok[0, base + c] = 1.0
    return pq, pk, oq, ok


def _fox_project(x_ref, sh_ref, sc_ref, wqkv_ref, bqkv_ref, wf_ref, bf_ref, tri_ref, k_ref, v_ref, lf_ref,
                 carry_ref, n_heads):
    bb, lt, d = x_ref.shape
    tm = bb * lt
    h = (x_ref[...] * (1.0 + sc_ref[...]) + sh_ref[...]).reshape(tm, d).astype(BF16)
    p = _dot(h, wqkv_ref[...]) + bqkv_ref[...]
    q, k, v = p[:, :d], p[:, d:2 * d], p[:, 2 * d:]
    k_ref[...] = k.reshape(bb, lt, d)
    v_ref[...] = v.reshape(bb, lt, d)

    pf = _dot(h, wf_ref[...]) + bf_ref[...]
    lane = lax.broadcasted_iota(jnp.int32, pf.shape, 1)
    lf = jnp.where(lane < n_heads, jnp.minimum(pf, 0.0) - jnp.log1p(jnp.exp(-jnp.abs(pf))), 0.0)
    lf_ref[...] = lf[:, :n_heads].reshape(bb, lt, n_heads)

    @pl.when(pl.program_id(1) == 0)
    def _():
        carry_ref[...] = jnp.zeros_like(carry_ref)

    tri = tri_ref[...]
    l3 = _split3(lf)
    cum = _dot(tri, l3[0]) + _dot(tri, l3[1]) + _dot(tri, l3[2]) + carry_ref[...]
    carry_ref[...] = cum[tm - 1:tm, :]
    qb = (q * (ATT_SCALE * LOG2E)).astype(BF16)
    return h, qb, k.astype(BF16), v, cum * LOG2E


def _inproj_fox_prompt_kernel(x_ref, sh_ref, sc_ref, wqkv_ref, bqkv_ref, wf_ref, bf_ref, tri_ref, wvt_ref, bvt_ref,
                              pq_ref, pk_ref, oq_ref, ok_ref, k_ref, v_ref, lf_ref, q_ref, kb_ref, vt_ref,
                              qaux_ref, kaux_ref, carry_ref, *, n_heads):
    bb, lt, d = x_ref.shape
    h, qb, kb, _, cum = _fox_project(x_ref, sh_ref, sc_ref, wqkv_ref, bqkv_ref, wf_ref, bf_ref, tri_ref,
                                     k_ref, v_ref, lf_ref, carry_ref, n_heads)
    q_ref[...] = qb.reshape(bb, lt, d)
    kb_ref[...] = kb.reshape(bb, lt, d)
    vt_ref[0] = (_dot_nt(wvt_ref[...], h) + bvt_ref[...]).astype(BF16)
    c3 = _split3(cum)
    qaux_ref[...] = (_dot3(c3, pq_ref) + oq_ref[...]).astype(BF16).reshape(bb, lt, LANES)
    kaux_ref[...] = (ok_ref[...] - _dot3(c3, pk_ref)).astype(BF16).reshape(bb, lt, LANES)


def _inproj_fox_sample_kernel(x_ref, sh_ref, sc_ref, wqkv_ref, bqkv_ref, wf_ref, bf_ref, tri_ref, eye_ref,
                              k_ref, v_ref, lf_ref, q_ref, kb_ref, vb_ref, cum_ref, cumt_ref, carry_ref, *, n_heads):
    bb, lt, d = x_ref.shape
    _, qb, kb, v, cum = _fox_project(x_ref, sh_ref, sc_ref, wqkv_ref, bqkv_ref, wf_ref, bf_ref, tri_ref,
                                     k_ref, v_ref, lf_ref, carry_ref, n_heads)
    vb = v.astype(BF16)
    for hd in range(n_heads):
        sl = slice(hd * HEAD_DIM, (hd + 1) * HEAD_DIM)
        q_ref[:, hd] = qb[:, sl].reshape(bb, lt, HEAD_DIM)
        kb_ref[:, hd] = kb[:, sl].reshape(bb, lt, HEAD_DIM)
        vb_ref[:, hd] = vb[:, sl].reshape(bb, lt, HEAD_DIM)
    cum_ref[...] = cum.reshape(bb, lt, LANES)
    c3 = _split3(cum)
    eye = eye_ref[...]
    cum_t = _dot_nt(eye, c3[0]) + _dot_nt(eye, c3[1]) + _dot_nt(eye, c3[2])
    for b in range(bb):
        cumt_ref[b] = cum_t[:, b * lt:(b + 1) * lt]


def _inproj_fox(x, sh, sc, wqkv, bqkv, wf, bf_, n_heads, sample):
    bsz, length, d = x.shape
    bb, lt = _row_tiles(bsz, length)
    tm = bb * lt
    r = np.arange(tm)
    tri = ((r[None, :] <= r[:, None]) & (r[None, :] // lt == r[:, None] // lt)).astype(np.float32)
    const = lambda shape: pl.BlockSpec(shape, lambda b, i: (0,) * len(shape))
    row = lambda w: pl.BlockSpec((bb, lt, w), lambda b, i: (b, i, 0))
    mod = pl.BlockSpec((bb, 1, d), lambda b, i: (b, 0, 0))
    shp = lambda w, dt: jax.ShapeDtypeStruct((bsz, length, w), dt)
    in_specs = [row(d), mod, mod, const((d, 3 * d)), const((1, 3 * d)), const((d, LANES)), const((1, LANES)),
                const((tm, tm))]
    args = [x, sh, sc, wqkv, bqkv, wf, bf_, jnp.asarray(tri, BF16)]
    out_specs = [row(d), row(d), row(n_heads)]
    out_shape = [shp(d, F32), shp(d, F32), shp(n_heads, F32)]
    if sample:
        heads = pl.BlockSpec((bb, n_heads, lt, HEAD_DIM), lambda b, i: (b, 0, i, 0))
        hshape = jax.ShapeDtypeStruct((bsz, n_heads, length, HEAD_DIM), BF16)
        body = _inproj_fox_sample_kernel
        in_specs += [const((LANES, LANES))]
        args += [jnp.eye(LANES, dtype=BF16)]
        out_specs += [heads, heads, heads, row(LANES), pl.BlockSpec((bb, LANES, lt), lambda b, i: (b, 0, i))]
        out_shape += [hshape, hshape, hshape, shp(LANES, F32), jax.ShapeDtypeStruct((bsz, LANES, length), F32)]
    else:
        pq, pk, oq, ok = _fox_aux_constants(n_heads)
        body = _inproj_fox_prompt_kernel
        in_specs += [const((d, d)), const((d, 1)), const((3, LANES, LANES)), const((3, LANES, LANES)),
                     const((1, LANES)), const((1, LANES))]
        args += [wqkv[:, 2 * d:].T, bqkv[0, 2 * d:].reshape(d, 1), jnp.asarray(pq, BF16), jnp.asarray(pk, BF16),
                 jnp.asarray(oq), jnp.asarray(ok)]
        out_specs += [row(d), row(d), pl.BlockSpec((1, d, lt), lambda b, i: (b, 0, i)), row(LANES), row(LANES)]
        out_shape += [shp(d, BF16), shp(d, BF16), jax.ShapeDtypeStruct((bsz, d, length), BF16),
                      shp(LANES, BF16), shp(LANES, BF16)]
    return pl.pallas_call(
        functools.partial(body, n_heads=n_heads),
        grid=(bsz // bb, length // lt),
        in_specs=in_specs,
        out_specs=out_specs,
        out_shape=out_shape,
        scratch_shapes=[pltpu.VMEM((1, LANES), F32)],
        compiler_params=_params("arbitrary", "arbitrary"),
        name="inproj_fox_sample" if sample else "inproj_fox_prompt",
    )(*args)


def _stacked_queries(q, qaux, pair):
    qf = q.astype(F32)
    af = qaux.astype(F32)
    lane = lax.broadcasted_iota(jnp.int32, qf.shape, 1)
    base = 2 * AUX_PER_HEAD * pair
    first = jnp.concatenate([jnp.where(lane < HEAD_DIM, qf, 0.0),
                             jnp.where((lane >= base) & (lane < base + AUX_PER_HEAD), af, 0.0)], axis=1)
    second = jnp.concatenate([jnp.where(lane >= HEAD_DIM, qf, 0.0),
                              jnp.where((lane >= base + AUX_PER_HEAD) & (lane < base + 2 * AUX_PER_HEAD), af, 0.0)],
                             axis=1)
    return jnp.concatenate([first, second], axis=0).astype(BF16)


def _fox_attn_kernel(q_ref, qaux_ref, k_ref, kaux_ref, vt_ref, o_ref, qs_ref, m_ref, l_ref, acc_ref):
    t = q_ref.shape[1]
    pairs = q_ref.shape[2] // LANES
    i = pl.program_id(1)
    qaux = qaux_ref[0]
    for p in range(pairs):
        qs_ref[p] = _stacked_queries(q_ref[0, :, p * LANES:(p + 1) * LANES], qaux, p)
    m_ref[...] = jnp.full(m_ref.shape, NEG, F32)
    l_ref[...] = jnp.zeros(l_ref.shape, F32)
    acc_ref[...] = jnp.zeros(acc_ref.shape, F32)

    def tile(j, mask):
        start = pl.multiple_of(j * t, t)
        kaux = kaux_ref[0, pl.ds(start, t), :]
        scores = []
        for p in range(pairs):
            kt = jnp.concatenate([k_ref[0, pl.ds(start, t), p * LANES:(p + 1) * LANES], kaux], axis=1)
            scores.append(_dot_nt(kt, qs_ref[p]))
        probs, alphas = [], []
        for p in range(pairs):
            s = scores[p] if mask is None else jnp.where(mask, scores[p], NEG)
            m = m_ref[p]
            m_new = jnp.maximum(m, jnp.max(s, axis=0, keepdims=True))
            alpha = jnp.exp2(m - m_new)
            pe = jnp.exp2(s - m_new)
            l_ref[p] = alpha * l_ref[p] + jnp.sum(pe, axis=0, keepdims=True)
            m_ref[p] = m_new
            probs.append(pe.astype(BF16))
            alphas.append(alpha)
        for p in range(pairs):
            pv = _dot(vt_ref[0, p * LANES:(p + 1) * LANES, pl.ds(start, t)], probs[p])
            acc_ref[p] = alphas[p] * acc_ref[p] + pv

    def body(j, carry):
        tile(j, None)
        return carry

    lax.fori_loop(0, i, body, 0)
    key = lax.broadcasted_iota(jnp.int32, (t, 2 * t), 0)
    query = lax.broadcasted_iota(jnp.int32, (t, 2 * t), 1)
    query = jnp.where(query >= t, query - t, query)
    tile(i, key <= query)
    row = lax.broadcasted_iota(jnp.int32, (LANES, t), 0)
    for p in range(pairs):
        out = acc_ref[p] / l_ref[p]
        o_ref[0, :, p * LANES:(p + 1) * LANES] = jnp.where(row < HEAD_DIM, out[:, :t], out[:, t:]).T.astype(BF16)


def _fox_attention(q, qaux, kb, kaux, vt):
    bsz, s, d = q.shape
    t = FOX_TILE
    pairs = d // LANES
    return pl.pallas_call(
        _fox_attn_kernel,
        grid=(bsz, s // t),
        in_specs=[pl.BlockSpec((1, t, d), lambda b, i: (b, i, 0)),
                  pl.BlockSpec((1, t, LANES), lambda b, i: (b, i, 0)),
                  pl.BlockSpec((1, s, d), lambda b, i: (b, 0, 0)),
                  pl.BlockSpec((1, s, LANES), lambda b, i: (b, 0, 0)),
                  pl.BlockSpec((1, d, s), lambda b, i: (b, 0, 0))],
        out_specs=pl.BlockSpec((1, t, d), lambda b, i: (b, i, 0)),
        out_shape=jax.ShapeDtypeStruct((bsz, s, d), BF16),
        scratch_shapes=[pltpu.VMEM((pairs, 2 * t, 2 * LANES), BF16),
                        pltpu.VMEM((pairs, 1, 2 * t), F32),
                        pltpu.VMEM((pairs, 1, 2 * t), F32),
                        pltpu.VMEM((pairs, LANES, 2 * t), F32)],
        compiler_params=_params("arbitrary", "arbitrary"),
        name="fox_attention",
    )(q, qaux, kb, kaux, vt)


def _cache_suffix_kernel(lf_ref, later_ref, suf_ref, carry_ref):
    @pl.when(pl.program_id(1) == 0)
    def _():
        carry_ref[...] = jnp.zeros_like(carry_ref)

    lf = lf_ref[0]
    later = later_ref[...]
    l3 = _split3(lf)
    carry = carry_ref[...]
    suf_ref[0] = (_dot(l3[0], later) + _dot(l3[1], later) + _dot(l3[2], later) + carry) * LOG2E
    carry_ref[...] = carry + jnp.sum(lf, axis=1, keepdims=True)


def _cache_suffix(clogf_t, layer):
    _, bsz, n_heads, plen = clogf_t.shape
    t = FOX_SUFFIX_TILE
    nt = plen // t
    r = np.arange(t)
    later = (r[:, None] > r[None, :]).astype(np.float32)
    return pl.pallas_call(
        _cache_suffix_kernel,
        grid=(bsz, nt),
        in_specs=[pl.BlockSpec((None, 1, n_heads, t), lambda b, j: (layer, b, 0, nt - 1 - j)),
                  pl.BlockSpec((t, t), lambda b, j: (0, 0))],
        out_specs=pl.BlockSpec((1, n_heads, t), lambda b, j: (b, 0, nt - 1 - j)),
        out_shape=jax.ShapeDtypeStruct((bsz, n_heads, plen), F32),
        scratch_shapes=[pltpu.VMEM((n_heads, 1), F32)],
        compiler_params=_params("arbitrary", "arbitrary"),
        name="fox_cache_suffix",
    )(clogf_t, jnp.asarray(later, BF16))


def _fox_sample_kernel(q_ref, cum_ref, kt_ref, vt_ref, suf_ref, kn_ref, vn_ref, cumt_ref, o_ref, m_ref, l_ref, acc_ref):
    step = pl.program_id(1)
    n_heads, n_new = q_ref.shape[1], q_ref.shape[2]

    @pl.when(step == 0)
    def _():
        m_ref[...] = jnp.full(m_ref.shape, NEG, F32)
        l_ref[...] = jnp.zeros(l_ref.shape, F32)
        acc_ref[...] = jnp.zeros(acc_ref.shape, F32)

    cum = cum_ref[0]

    def update(scores, value_dot):
        probs, alphas = [], []
        for h in range(n_heads):
            s = scores[h]
            m_old = m_ref[h]
            m_new = jnp.maximum(m_old, jnp.max(s, axis=1, keepdims=True))
            alpha = jnp.exp2(m_old - m_new)
            pe = jnp.exp2(s - m_new)
            l_ref[h] = alpha * l_ref[h] + jnp.sum(pe, axis=1, keepdims=True)
            m_ref[h] = m_new
            probs.append(pe.astype(BF16))
            alphas.append(alpha)
        for h in range(n_heads):
            acc_ref[h] = alphas[h] * acc_ref[h] + value_dot(h, probs[h])

    update([_dot(q_ref[0, h], kt_ref[h].astype(BF16)) + cum[:, h:h + 1] + suf_ref[0, h:h + 1, :]
            for h in range(n_heads)],
           lambda h, pr: _dot_nt(pr, vt_ref[h].astype(BF16)))

    @pl.when(step == pl.num_programs(1) - 1)
    def _():
        row = lax.broadcasted_iota(jnp.int32, (n_new, n_new), 0)
        col = lax.broadcasted_iota(jnp.int32, (n_new, n_new), 1)
        update([jnp.where(col <= row,
                          _dot_nt(q_ref[0, h], kn_ref[0, h]) + cum[:, h:h + 1] - cumt_ref[0, h:h + 1, :], NEG)
                for h in range(n_heads)],
               lambda h, pr: _dot(pr, vn_ref[0, h]))
        for h in range(n_heads):
            o_ref[0, :, h * HEAD_DIM:(h + 1) * HEAD_DIM] = (acc_ref[h] / l_ref[h]).astype(BF16)


def _fox_sample_attention(q, cum, cache_kt, cache_vt, layer, suffix, kn, vn, cum_t):
    bsz, n_heads, n_new, _ = q.shape
    plen = cache_kt.shape[-1]
    t = FOX_CACHE_TILE
    heads = pl.BlockSpec((1, n_heads, n_new, HEAD_DIM), lambda b, j: (b, 0, 0, 0))
    cache = pl.BlockSpec((None, None, n_heads, HEAD_DIM, t), lambda b, j: (layer, b, 0, 0, j))
    return pl.pallas_call(
        _fox_sample_kernel,
        grid=(bsz, plen // t),
        in_specs=[heads, pl.BlockSpec((1, n_new, LANES), lambda b, j: (b, 0, 0)), cache, cache,
                  pl.BlockSpec((1, n_heads, t), lambda b, j: (b, 0, j)), heads, heads,
                  pl.BlockSpec((1, LANES, n_new), lambda b, j: (b, 0, 0))],
        out_specs=pl.BlockSpec((1, n_new, n_heads * HEAD_DIM), lambda b, j: (b, 0, 0)),
        out_shape=jax.ShapeDtypeStruct((bsz, n_new, n_heads * HEAD_DIM), BF16),
        scratch_shapes=[pltpu.VMEM((n_heads, n_new, 1), F32),
                        pltpu.VMEM((n_heads, n_new, 1), F32),
                        pltpu.VMEM((n_heads, n_new, HEAD_DIM), F32)],
        compiler_params=_params("arbitrary", "arbitrary"),
        name="fox_sample_attention",
    )(q, cum, cache_kt, cache_vt, suffix, kn, vn, cum_t)


def _rope_tables(positions):
    half = HEAD_DIM // 2
    inv = ROPE_THETA ** (-jnp.arange(half, dtype=F32) / half)
    ang = positions.astype(F32)[:, None] * inv[None, :]
    cos, sin = jnp.cos(ang), jnp.sin(ang)
    reps = LANES // HEAD_DIM
    return jnp.tile(jnp.concatenate([cos, cos], axis=1), (1, reps)), jnp.tile(jnp.concatenate([-sin, sin], axis=1), (1, reps))


def _rope(x, cos, sin_signed):
    n = x.shape[1]
    reps = n // LANES
    lane = lax.broadcasted_iota(jnp.int32, x.shape, 1)
    half = HEAD_DIM // 2
    partner = jnp.where((lane & (HEAD_DIM - 1)) < half, pltpu.roll(x, n - half, 1), pltpu.roll(x, half, 1))
    return x * jnp.concatenate([cos] * reps, axis=1) + partner * jnp.concatenate([sin_signed] * reps, axis=1)


def _inproj_swa_kernel(x_ref, sh_ref, sc_ref, w_ref, b_ref, cos_ref, sin_ref, q_ref, kb_ref, vb_ref, k_ref, v_ref):
    bb, lt, d = x_ref.shape
    tm = bb * lt
    nq = q_ref.shape[1] * HEAD_DIM
    nkv = kb_ref.shape[1] * HEAD_DIM
    h = (x_ref[...] * (1.0 + sc_ref[...]) + sh_ref[...]).reshape(tm, d).astype(BF16)
    p = _dot(h, w_ref[...]) + b_ref[...]
    cos = jnp.concatenate([cos_ref[...]] * bb, axis=0)
    sin = jnp.concatenate([sin_ref[...]] * bb, axis=0)
    q = _rope(p[:, :nq], cos, sin)
    k = _rope(p[:, nq:nq + nkv], cos, sin)
    v = p[:, nq + nkv:]
    k_ref[...] = k.reshape(bb, lt, nkv)
    v_ref[...] = v.reshape(bb, lt, nkv)
    qb = (q * ATT_SCALE).astype(BF16)
    kb = k.astype(BF16)
    vb = v.astype(BF16)
    for hd in range(nq // HEAD_DIM):
        q_ref[:, hd] = qb[:, hd * HEAD_DIM:(hd + 1) * HEAD_DIM].reshape(bb, lt, HEAD_DIM)
    for hd in range(nkv // HEAD_DIM):
        kb_ref[:, hd] = kb[:, hd * HEAD_DIM:(hd + 1) * HEAD_DIM].reshape(bb, lt, HEAD_DIM)
        vb_ref[:, hd] = vb[:, hd * HEAD_DIM:(hd + 1) * HEAD_DIM].reshape(bb, lt, HEAD_DIM)


def _inproj_swa(x, sh, sc, w, b, positions, n_q_heads):
    bsz, length, d = x.shape
    n = w.shape[1]
    n_kv_heads = n_q_heads // SWA_GROUP
    nkv = n_kv_heads * HEAD_DIM
    bb, lt = _row_tiles(bsz, length)
    cos, sin = _rope_tables(positions)
    const = lambda shape: pl.BlockSpec(shape, lambda bi, i: (0,) * len(shape))
    mod = pl.BlockSpec((bb, 1, d), lambda bi, i: (bi, 0, 0))
    tab = pl.BlockSpec((lt, LANES), lambda bi, i: (i, 0))
    heads = lambda nh: pl.BlockSpec((bb, nh, lt, HEAD_DIM), lambda bi, i: (bi, 0, i, 0))
    rows = lambda wd: pl.BlockSpec((bb, lt, wd), lambda bi, i: (bi, i, 0))
    return pl.pallas_call(
        _inproj_swa_kernel,
        grid=(bsz // bb, length // lt),
        in_specs=[rows(d), mod, mod, const((d, n)), const((1, n)), tab, tab],
        out_specs=[heads(n_q_heads), heads(n_kv_heads), heads(n_kv_heads), rows(nkv), rows(nkv)],
        out_shape=[jax.ShapeDtypeStruct((bsz, n_q_heads, length, HEAD_DIM), BF16),
                   jax.ShapeDtypeStruct((bsz, n_kv_heads, length, HEAD_DIM), BF16),
                   jax.ShapeDtypeStruct((bsz, n_kv_heads, length, HEAD_DIM), BF16),
                   jax.ShapeDtypeStruct((bsz, length, nkv), F32),
                   jax.ShapeDtypeStruct((bsz, length, nkv), F32)],
        compiler_params=_params("arbitrary", "arbitrary"),
        name="inproj_swa",
    )(x, sh, sc, w, b, cos, sin)


def _swa_attn_kernel(sinks_ref, q_ref, k_ref, v_ref, o_ref, *, q_offset):
    g, tq, _ = q_ref.shape[1:]
    wn = tq + WINDOW
    kvh = pl.program_id(1)
    q0 = q_offset + pl.program_id(2) * tq
    ws = pl.multiple_of(jnp.maximum(q0 - WINDOW, 0), CHUNK)
    kw = k_ref[0, 0, pl.ds(ws, wn), :]
    vw = v_ref[0, 0, pl.ds(ws, wn), :]
    shift = CHUNK.bit_length() - 1
    qc = jnp.right_shift(q0 + lax.broadcasted_iota(jnp.int32, (tq, wn), 0), shift)
    kc = jnp.right_shift(ws + lax.broadcasted_iota(jnp.int32, (tq, wn), 1), shift)
    visible = (kc <= qc) & (kc >= qc - WINDOW // CHUNK)
    for i in range(g):
        sink = sinks_ref[kvh * g + i]
        s = jnp.where(visible, _dot_nt(q_ref[0, i], kw), NEG)
        m = jnp.maximum(jnp.max(s, axis=1, keepdims=True), sink)
        pe = jnp.exp(s - m)
        denom = jnp.sum(pe, axis=1, keepdims=True) + jnp.exp(sink - m)
        o = _dot(pe.astype(BF16), vw) / denom
        o_ref[0, :, i * HEAD_DIM:(i + 1) * HEAD_DIM] = o.astype(BF16)


def _swa_attention(q, k, v, sinks, q_offset):
    bsz, nh, lq, _ = q.shape
    nkv, lk = k.shape[1], k.shape[2]
    g = nh // nkv
    tq = min(SWA_TILE, lq)
    kv = pl.BlockSpec((1, 1, lk, HEAD_DIM), lambda b, h, i, s: (b, h, 0, 0))
    return pl.pallas_call(
        functools.partial(_swa_attn_kernel, q_offset=q_offset),
        grid_spec=pltpu.PrefetchScalarGridSpec(
            num_scalar_prefetch=1,
            grid=(bsz, nkv, lq // tq),
            in_specs=[pl.BlockSpec((1, g, tq, HEAD_DIM), lambda b, h, i, s: (b, h, i, 0)), kv, kv],
            out_specs=pl.BlockSpec((1, tq, g * HEAD_DIM), lambda b, h, i, s: (b, i, h))),
        out_shape=jax.ShapeDtypeStruct((bsz, lq, nh * HEAD_DIM), BF16),
        compiler_params=_params("arbitrary", "arbitrary", "arbitrary"),
        name="swa_attention",
    )(sinks, q, k, v)


def _route(r):
    lane = lax.broadcasted_iota(jnp.int32, r.shape, 1).astype(F32)
    big = float(LANES)
    gl = jnp.where(lane < N_GROUPS, r, NEG)
    gmax = jnp.max(gl, axis=1, keepdims=True)
    gp = 1.0 / jnp.sum(jnp.exp(gl - gmax), axis=1, keepdims=True)
    gi = jnp.min(jnp.where(gl == gmax, lane, big), axis=1, keepdims=True)
    lo = ROUTE_GATE_LANE + EXPERTS_PER_GROUP * gi
    el = jnp.where((lane >= lo) & (lane < lo + EXPERTS_PER_GROUP), r, NEG)
    e1 = jnp.max(el, axis=1, keepdims=True)
    esum = jnp.sum(jnp.exp(el - e1), axis=1, keepdims=True)
    i1 = jnp.min(jnp.where(el == e1, lane, big), axis=1, keepdims=True)
    el2 = jnp.where(lane == i1, NEG, el)
    e2 = jnp.max(el2, axis=1, keepdims=True)
    i2 = jnp.min(jnp.where(el2 == e2, lane, big), axis=1, keepdims=True)
    p1 = 1.0 / esum
    p2 = jnp.exp(e2 - e1) / esum
    w1 = gp * p1 / (p1 + p2)
    w2 = gp * p2 / (p1 + p2)
    gates = jnp.where(lane == i1, w1, jnp.where(lane == i2, w2, 0.0))
    return jnp.where(lane == ROUTE_GROUP_LANE, gi, gates)


def _outproj_kernel(x_ref, o_ref, w_ref, g1_ref, lng_ref, lnb_ref, sh_ref, sc_ref, wrh_ref, wrl_ref, br_ref,
                    x1_ref, h2_ref, route_ref, *, alpha):
    bb, lt, d = x_ref.shape
    tm = bb * lt
    a = _dot(o_ref[...].reshape(tm, d), w_ref[...]).reshape(bb, lt, d)
    x1 = _layer_norm(alpha * x_ref[...] + (1.0 + g1_ref[...]) * a, lng_ref[...], lnb_ref[...])
    x1_ref[...] = x1
    h2 = (x1 * (1.0 + sc_ref[...]) + sh_ref[...]).reshape(tm, d)
    hi = h2.astype(BF16)
    lo = (h2 - hi.astype(F32)).astype(BF16)
    h2_ref[...] = hi.reshape(bb, lt, d)
    r = _dot(hi, wrh_ref[...]) + _dot(hi, wrl_ref[...]) + _dot(lo, wrh_ref[...]) + br_ref[...]
    route_ref[...] = _route(r).reshape(bb, lt, LANES)


def _outproj(x, o, w_out, g1, lng, lnb, sh2, sc2, wr_hi, wr_lo, br, alpha):
    bsz, length, d = x.shape
    bb, lt = _row_tiles(bsz, length)
    const = lambda shape: pl.BlockSpec(shape, lambda b, i: (0,) * len(shape))
    rows = lambda w: pl.BlockSpec((bb, lt, w), lambda b, i: (b, i, 0))
    mod = pl.BlockSpec((bb, 1, d), lambda b, i: (b, 0, 0))
    return pl.pallas_call(
        functools.partial(_outproj_kernel, alpha=alpha),
        grid=(bsz // bb, length // lt),
        in_specs=[rows(d), rows(d), const((d, d)), mod, const((1, d)), const((1, d)), mod, mod,
                  const((d, LANES)), const((d, LANES)), const((1, LANES))],
        out_specs=[rows(d), rows(d), rows(LANES)],
        out_shape=[jax.ShapeDtypeStruct((bsz, length, d), F32), jax.ShapeDtypeStruct((bsz, length, d), BF16),
                   jax.ShapeDtypeStruct((bsz, length, LANES), F32)],
        compiler_params=_params("arbitrary", "arbitrary"),
        name="outproj_router",
    )(x, o, w_out, g1, lng, lnb, sh2, sc2, wr_hi, wr_lo, br)


def _moe_kernel(h_ref, route_ref, wg_ref, wu_ref, wd_ref, x1_ref, g2_ref, lng_ref, lnb_ref, x2_ref, acc_ref,
                *, alpha):
    bb, lt, d = h_ref.shape
    tm = bb * lt
    grp = pl.program_id(2)

    @pl.when(grp == 0)
    def _():
        acc_ref[...] = jnp.zeros_like(acc_ref)

    h = h_ref[...].reshape(tm, d)
    route = route_ref[...].reshape(tm, LANES)
    lane = lax.broadcasted_iota(jnp.int32, route.shape, 1)
    base = ROUTE_GATE_LANE + EXPERTS_PER_GROUP * grp
    acts = []
    for e in range(EXPERTS_PER_GROUP):
        gate = jnp.sum(jnp.where(lane == base + e, route, 0.0), axis=1, keepdims=True)
        hg = _dot(h, wg_ref[e])
        hu = _dot(h, wu_ref[e])
        acts.append(((hg / (1.0 + jnp.exp(-hg))) * hu * gate).astype(BF16))
    a = jnp.concatenate(acts, axis=1)
    acc_ref[...] += _dot(a, wd_ref[...].reshape(a.shape[1], d))

    @pl.when(grp == pl.num_programs(2) - 1)
    def _():
        y = alpha * x1_ref[...] + (1.0 + g2_ref[...]) * acc_ref[...].reshape(bb, lt, d)
        x2_ref[...] = _layer_norm(y, lng_ref[...], lnb_ref[...])


def _moe(h2, route, wg, wu, wd, layer, x1, g2, lng, lnb, alpha):
    bsz, length, d = h2.shape
    f = wg.shape[-1]
    bb, lt = _row_tiles(bsz, length)
    const = lambda shape: pl.BlockSpec(shape, lambda b, i, g: (0,) * len(shape))
    rows = lambda w: pl.BlockSpec((bb, lt, w), lambda b, i, g: (b, i, 0))
    mod = pl.BlockSpec((bb, 1, d), lambda b, i, g: (b, 0, 0))
    epg = EXPERTS_PER_GROUP
    return pl.pallas_call(
        functools.partial(_moe_kernel, alpha=alpha),
        grid=(bsz // bb, length // lt, N_GROUPS),
        in_specs=[rows(d), rows(LANES),
                  pl.BlockSpec((None, epg, d, f), lambda b, i, g: (layer, g, 0, 0)),
                  pl.BlockSpec((None, epg, d, f), lambda b, i, g: (layer, g, 0, 0)),
                  pl.BlockSpec((None, epg, f, d), lambda b, i, g: (layer, g, 0, 0)),
                  rows(d), mod, const((1, d)), const((1, d))],
        out_specs=rows(d),
        out_shape=jax.ShapeDtypeStruct((bsz, length, d), F32),
        scratch_shapes=[pltpu.VMEM((bb * lt, d), F32)],
        compiler_params=_params("arbitrary", "arbitrary", "arbitrary"),
        name="moe_deepnorm",
    )(h2, route, wg, wu, wd, x1, g2, lng, lnb)


def kernel(x_prompt, x_sample, cache_fox_k, cache_fox_v, cache_fox_logf, cache_swa_k, cache_swa_v, c_prompt, c_sample, mod_w, mod_b, ln1_g, ln1_b, ln2_g, ln2_b, fox_w_in, fox_b_in, fox_w_out, swa_w_in, swa_b_in, swa_sinks, swa_w_out, router_g_w, router_g_b, router_e_w, router_e_b, w_gate, w_up, w_down):
    depth = mod_w.shape[0]
    bp, seq, d = x_prompt.shape
    bs, dec = x_sample.shape[:2]
    past = cache_fox_k.shape[2]
    fox_heads = cache_fox_k.shape[3]
    swa_kv = cache_swa_k.shape[3]
    swa_heads = swa_kv * SWA_GROUP
    keep = cache_swa_k.shape[2]
    alpha = (2.0 * depth) ** 0.25

    n_c = bp + bs
    c_all = jnp.pad(jnp.concatenate([c_prompt, c_sample], axis=0), ((0, -n_c % 16), (0, 0)))
    mods = _modulation(c_all, mod_w, mod_b)

    def mod_parts(l, lo, hi):
        return [mods[l, lo:hi, i * d:(i + 1) * d].reshape(hi - lo, 1, d) for i in range(6)]

    ck_t = jnp.transpose(cache_fox_k, (0, 1, 3, 4, 2))
    cv_t = jnp.transpose(cache_fox_v, (0, 1, 3, 4, 2))
    clogf_t = jnp.transpose(cache_fox_logf, (0, 1, 3, 2))
    wg_b, wu_b, wd_b = w_gate.astype(BF16), w_up.astype(BF16), w_down.astype(BF16)
    n_grp = router_g_w.shape[-1]
    n_exp = w_gate.shape[1]

    xp, xs = x_prompt, x_sample
    fkp, fvp, flp, skp, svp = [], [], [], [], []
    fks, fvs, fls, sks, svs = [], [], [], [], []
    for l in range(depth):
        j = l // 2
        mp = mod_parts(l, 0, bp)
        ms = mod_parts(l, bp, n_c)
        if l % 2 == 0:
            n_qkv = 3 * d
            wqkv = fox_w_in[j, :, :n_qkv].astype(BF16)
            bqkv = fox_b_in[j, :n_qkv].reshape(1, n_qkv)
            wf = jnp.pad(fox_w_in[j, :, n_qkv:], ((0, 0), (0, LANES - fox_heads))).astype(BF16)
            bf_ = jnp.pad(fox_b_in[j, n_qkv:], (0, LANES - fox_heads)).reshape(1, LANES)
            k_p, v_p, lf_p, q, kb, vt, qaux, kaux = _inproj_fox(xp, mp[0], mp[1], wqkv, bqkv, wf, bf_, fox_heads, False)
            op = _fox_attention(q, qaux, kb, kaux, vt)
            k_s, v_s, lf_s, q, kb, vb, cum, cum_t = _inproj_fox(xs, ms[0], ms[1], wqkv, bqkv, wf, bf_, fox_heads, True)
            suffix = _cache_suffix(clogf_t, j)
            os_ = _fox_sample_attention(q, cum, ck_t, cv_t, j, suffix, kb, vb, cum_t)
            hshape = (fox_heads, HEAD_DIM)
            fkp.append(k_p.reshape(bp, seq, *hshape)); fvp.append(v_p.reshape(bp, seq, *hshape)); flp.append(lf_p)
            fks.append(k_s.reshape(bs, dec, *hshape)); fvs.append(v_s.reshape(bs, dec, *hshape)); fls.append(lf_s)
            w_out = fox_w_out[j].astype(BF16)
        else:
            w_in = swa_w_in[j].astype(BF16)
            b_in = swa_b_in[j].reshape(1, -1)
            q, kb, vb, k_p, v_p = _inproj_swa(xp, mp[0], mp[1], w_in, b_in, jnp.arange(seq), swa_heads)
            op = _swa_attention(q, kb, vb, swa_sinks[j], 0)
            q, kb, vb, k_s, v_s = _inproj_swa(xs, ms[0], ms[1], w_in, b_in, past + jnp.arange(dec), swa_heads)
            hshape = (swa_kv, HEAD_DIM)
            k_all = jnp.concatenate([cache_swa_k[j], k_s.reshape(bs, dec, *hshape)], axis=1)
            v_all = jnp.concatenate([cache_swa_v[j], v_s.reshape(bs, dec, *hshape)], axis=1)
            os_ = _swa_attention(q, jnp.swapaxes(k_all, 1, 2).astype(BF16), jnp.swapaxes(v_all, 1, 2).astype(BF16),
                                 swa_sinks[j], keep)
            skp.append(k_p[:, seq - WINDOW:].reshape(bp, WINDOW, *hshape))
            svp.append(v_p[:, seq - WINDOW:].reshape(bp, WINDOW, *hshape))
            sks.append(k_all[:, k_all.shape[1] - keep:]); svs.append(v_all[:, v_all.shape[1] - keep:])
            w_out = swa_w_out[j].astype(BF16)

        wr = jnp.concatenate([router_g_w[l], jnp.moveaxis(router_e_w[l], 0, 1).reshape(d, n_exp)], axis=1)
        wr = jnp.pad(wr, ((0, 0), (0, LANES - n_grp - n_exp)))
        wr_hi = wr.astype(BF16)
        wr_lo = (wr - wr_hi.astype(F32)).astype(BF16)
        br = jnp.pad(jnp.concatenate([router_g_b[l], router_e_b[l].reshape(-1)]), (0, LANES - n_grp - n_exp))
        br = br.reshape(1, LANES)
        lng1, lnb1 = ln1_g[l].reshape(1, d), ln1_b[l].reshape(1, d)
        lng2, lnb2 = ln2_g[l].reshape(1, d), ln2_b[l].reshape(1, d)
        for grp_in in ("prompt", "sample"):
            x, o, m = (xp, op, mp) if grp_in == "prompt" else (xs, os_, ms)
            x1, h2, route = _outproj(x, o, w_out, m[2], lng1, lnb1, m[3], m[4], wr_hi, wr_lo, br, alpha)
            x2 = _moe(h2, route, wg_b, wu_b, wd_b, l, x1, m[5], lng2, lnb2, alpha)
            if grp_in == "prompt":
                xp = x2
            else:
                xs = x2
    return (xp, xs,
            jnp.stack(fkp), jnp.stack(fvp), jnp.stack(flp), jnp.stack(skp), jnp.stack(svp),
            jnp.stack(fks), jnp.stack(fvs), jnp.stack(fls), jnp.stack(sks), jnp.stack(svs))
```

```python
import functools

import numpy as np
import jax
import jax.numpy as jnp
from jax import lax
from jax.experimental import pallas as pl
from jax.experimental.pallas import tpu as pltpu

F32 = jnp.float32
BF16 = jnp.bfloat16

HEAD_DIM = 64
CHUNK = 64
WINDOW = 128
SWA_GROUP = 4
N_GROUPS = 4
EXPERTS_PER_GROUP = 4
ROPE_THETA = 10000.0
LN_EPS = 1e-5
ATT_SCALE = HEAD_DIM ** -0.5
NEG = -1e30
LOG2E = 1.4426950408889634

LANES = 128
VMEM_LIMIT = 56 * 1024 * 1024
ROW_TILE = 512
FOX_TILE = 256
FOX_CACHE_TILE = 1024
FOX_SUFFIX_TILE = 512
SWA_TILE = 256
AUX_PER_HEAD = 6
ROUTE_GROUP_LANE = 0
ROUTE_GATE_LANE = 4


def _dot(a, b):
    return jnp.dot(a, b, preferred_element_type=F32)


def _dot_nt(a, b):
    return lax.dot_general(a, b, (((1,), (1,)), ((), ())), preferred_element_type=F32)


def _split3(x):
    hi = x.astype(BF16)
    r = x - hi.astype(F32)
    mid = r.astype(BF16)
    lo = (r - mid.astype(F32)).astype(BF16)
    return hi, mid, lo


def _dot3(a3, b):
    return _dot(a3[0], b[0]) + _dot(a3[1], b[1]) + _dot(a3[2], b[2])


def _layer_norm(y, g, b):
    mu = jnp.mean(y, axis=-1, keepdims=True)
    d = y - mu
    var = jnp.mean(d * d, axis=-1, keepdims=True)
    return d * lax.rsqrt(var + LN_EPS) * g + b


def _params(*sem):
    return pltpu.CompilerParams(dimension_semantics=sem, vmem_limit_bytes=VMEM_LIMIT)


def _row_tiles(bsz, length):
    if length >= ROW_TILE:
        return 1, ROW_TILE
    return ROW_TILE // length, length


def _mod_kernel(c_ref, w_ref, b_ref, o_ref):
    c = c_ref[...]
    s = (c / (1.0 + jnp.exp(-c))).astype(BF16)
    o_ref[0] = _dot(s, w_ref[0].astype(BF16)) + b_ref[0]


def _modulation(c_all, mod_w, mod_b):
    depth, d, n = mod_w.shape
    rows = c_all.shape[0]
    tn = n // 4
    return pl.pallas_call(
        _mod_kernel,
        grid=(depth, n // tn),
        in_specs=[pl.BlockSpec((rows, d), lambda l, j: (0, 0)),
                  pl.BlockSpec((1, d, tn), lambda l, j: (l, 0, j)),
                  pl.BlockSpec((1, 1, tn), lambda l, j: (l, 0, j))],
        out_specs=pl.BlockSpec((1, rows, tn), lambda l, j: (l, 0, j)),
        out_shape=jax.ShapeDtypeStruct((depth, rows, n), F32),
        compiler_params=_params("arbitrary", "arbitrary"),
        name="modulation",
    )(c_all, mod_w, mod_b.reshape(depth, 1, n))


def _fox_aux_constants(n_heads):
    pq = np.zeros((3, LANES, LANES), np.float32)
    pk = np.zeros((3, LANES, LANES), np.float32)
    oq = np.zeros((1, LANES), np.float32)
    ok = np.zeros((1, LANES), np.float32)
    for h in range(n_heads):
        base = AUX_PER_HEAD * h
        for c in range(3):
            pq[c, h, base + c] = 1.0
            pk[c, h, base + 3 + c] = 1.0
            oq[0, base + 3 + c] = 1.0
            ok[0, base + c] = 1.0
    return pq, pk, oq, ok


def _fox_project(x_ref, sh_ref, sc_ref, wqkv_ref, bqkv_ref, wf_ref, bf_ref, tri_ref, k_ref, v_ref, lf_ref,
                 carry_ref, n_heads):
    bb, lt, d = x_ref.shape
    tm = bb * lt
    h = (x_ref[...] * (1.0 + sc_ref[...]) + sh_ref[...]).reshape(tm, d).astype(BF16)
    p = _dot(h, wqkv_ref[...]) + bqkv_ref[...]
    q, k, v = p[:, :d], p[:, d:2 * d], p[:, 2 * d:]
    k_ref[...] = k.reshape(bb, lt, d)
    v_ref[...] = v.reshape(bb, lt, d)

    pf = _dot(h, wf_ref[...]) + bf_ref[...]
    lane = lax.broadcasted_iota(jnp.int32, pf.shape, 1)
    lf = jnp.where(lane < n_heads, jnp.minimum(pf, 0.0) - jnp.log1p(jnp.exp(-jnp.abs(pf))), 0.0)
    lf_ref[...] = lf[:, :n_heads].reshape(bb, lt, n_heads)

    @pl.when(pl.program_id(1) == 0)
    def _():
        carry_ref[...] = jnp.zeros_like(carry_ref)

    tri = tri_ref[...]
    l3 = _split3(lf)
    cum = _dot(tri, l3[0]) + _dot(tri, l3[1]) + _dot(tri, l3[2]) + carry_ref[...]
    carry_ref[...] = cum[tm - 1:tm, :]
    qb = (q * (ATT_SCALE * LOG2E)).astype(BF16)
    return h, qb, k.astype(BF16), v, cum * LOG2E


def _inproj_fox_prompt_kernel(x_ref, sh_ref, sc_ref, wqkv_ref, bqkv_ref, wf_ref, bf_ref, tri_ref, wvt_ref, bvt_ref,
                              pq_ref, pk_ref, oq_ref, ok_ref, k_ref, v_ref, lf_ref, q_ref, kb_ref, vt_ref,
                              qaux_ref, kaux_ref, carry_ref, *, n_heads):
    bb, lt, d = x_ref.shape
    h, qb, kb, _, cum = _fox_project(x_ref, sh_ref, sc_ref, wqkv_ref, bqkv_ref, wf_ref, bf_ref, tri_ref,
                                     k_ref, v_ref, lf_ref, carry_ref, n_heads)
    q_ref[...] = qb.reshape(bb, lt, d)
    kb_ref[...] = kb.reshape(bb, lt, d)
    vt_ref[0] = (_dot_nt(wvt_ref[...], h) + bvt_ref[...]).astype(BF16)
    c3 = _split3(cum)
    qaux_ref[...] = (_dot3(c3, pq_ref) + oq_ref[...]).astype(BF16).reshape(bb, lt, LANES)
    kaux_ref[...] = (ok_ref[...] - _dot3(c3, pk_ref)).astype(BF16).reshape(bb, lt, LANES)


def _inproj_fox_sample_kernel(x_ref, sh_ref, sc_ref, wqkv_ref, bqkv_ref, wf_ref, bf_ref, tri_ref, eye_ref,
                              k_ref, v_ref, lf_ref, q_ref, kb_ref, vb_ref, cum_ref, cumt_ref, carry_ref, *, n_heads):
    bb, lt, d = x_ref.shape
    _, qb, kb, v, cum = _fox_project(x_ref, sh_ref, sc_ref, wqkv_ref, bqkv_ref, wf_ref, bf_ref, tri_ref,
                                     k_ref, v_ref, lf_ref, carry_ref, n_heads)
    vb = v.astype(BF16)
    for hd in range(n_heads):
        sl = slice(hd * HEAD_DIM, (hd + 1) * HEAD_DIM)
        q_ref[:, hd] = qb[:, sl].reshape(bb, lt, HEAD_DIM)
        kb_ref[:, hd] = kb[:, sl].reshape(bb, lt, HEAD_DIM)
        vb_ref[:, hd] = vb[:, sl].reshape(bb, lt, HEAD_DIM)
    cum_ref[...] = cum.reshape(bb, lt, LANES)
    c3 = _split3(cum)
    eye = eye_ref[...]
    cum_t = _dot_nt(eye, c3[0]) + _dot_nt(eye, c3[1]) + _dot_nt(eye, c3[2])
    for b in range(bb):
        cumt_ref[b] = cum_t[:, b * lt:(b + 1) * lt]


def _inproj_fox(x, sh, sc, wqkv, bqkv, wf, bf_, n_heads, sample):
    bsz, length, d = x.shape
    bb, lt = _row_tiles(bsz, length)
    tm = bb * lt
    r = np.arange(tm)
    tri = ((r[None, :] <= r[:, None]) & (r[None, :] // lt == r[:, None] // lt)).astype(np.float32)
    const = lambda shape: pl.BlockSpec(shape, lambda b, i: (0,) * len(shape))
    row = lambda w: pl.BlockSpec((bb, lt, w), lambda b, i: (b, i, 0))
    mod = pl.BlockSpec((bb, 1, d), lambda b, i: (b, 0, 0))
    shp = lambda w, dt: jax.ShapeDtypeStruct((bsz, length, w), dt)
    in_specs = [row(d), mod, mod, const((d, 3 * d)), const((1, 3 * d)), const((d, LANES)), const((1, LANES)),
                const((tm, tm))]
    args = [x, sh, sc, wqkv, bqkv, wf, bf_, jnp.asarray(tri, BF16)]
    out_specs = [row(d), row(d), row(n_heads)]
    out_shape = [shp(d, F32), shp(d, F32), shp(n_heads, F32)]
    if sample:
        heads = pl.BlockSpec((bb, n_heads, lt, HEAD_DIM), lambda b, i: (b, 0, i, 0))
        hshape = jax.ShapeDtypeStruct((bsz, n_heads, length, HEAD_DIM), BF16)
        body = _inproj_fox_sample_kernel
        in_specs += [const((LANES, LANES))]
        args += [jnp.eye(LANES, dtype=BF16)]
        out_specs += [heads, heads, heads, row(LANES), pl.BlockSpec((bb, LANES, lt), lambda b, i: (b, 0, i))]
        out_shape += [hshape, hshape, hshape, shp(LANES, F32), jax.ShapeDtypeStruct((bsz, LANES, length), F32)]
    else:
        pq, pk, oq, ok = _fox_aux_constants(n_heads)
        body = _inproj_fox_prompt_kernel
        in_specs += [const((d, d)), const((d, 1)), const((3, LANES, LANES)), const((3, LANES, LANES)),
                     const((1, LANES)), const((1, LANES))]
        args += [wqkv[:, 2 * d:].T, bqkv[0, 2 * d:].reshape(d, 1), jnp.asarray(pq, BF16), jnp.asarray(pk, BF16),
                 jnp.asarray(oq), jnp.asarray(ok)]
        out_specs += [row(d), row(d), pl.BlockSpec((1, d, lt), lambda b, i: (b, 0, i)), row(LANES), row(LANES)]
        out_shape += [shp(d, BF16), shp(d, BF16), jax.ShapeDtypeStruct((bsz, d, length), BF16),
                      shp(LANES, BF16), shp(LANES, BF16)]
    return pl.pallas_call(
        functools.partial(body, n_heads=n_heads),
        grid=(bsz // bb, length // lt),
        in_specs=in_specs,
        out_specs=out_specs,
        out_shape=out_shape,
        scratch_shapes=[pltpu.VMEM((1, LANES), F32)],
        compiler_params=_params("arbitrary", "arbitrary"),
        name="inproj_fox_sample" if sample else "inproj_fox_prompt",
    )(*args)


def _stacked_queries(q, qaux, pair):
    qf = q.astype(F32)
    af = qaux.astype(F32)
    lane = lax.broadcasted_iota(jnp.int32, qf.shape, 1)
    base = 2 * AUX_PER_HEAD * pair
    first = jnp.concatenate([jnp.where(lane < HEAD_DIM, qf, 0.0),
                             jnp.where((lane >= base) & (lane < base + AUX_PER_HEAD), af, 0.0)], axis=1)
    second = jnp.concatenate([jnp.where(lane >= HEAD_DIM, qf, 0.0),
                              jnp.where((lane >= base + AUX_PER_HEAD) & (lane < base + 2 * AUX_PER_HEAD), af, 0.0)],
                             axis=1)
    return jnp.concatenate([first, second], axis=0).astype(BF16)


def _fox_attn_kernel(q_ref, qaux_ref, k_ref, kaux_ref, vt_ref, o_ref, qs_ref, m_ref, l_ref, acc_ref):
    t = q_ref.shape[1]
    pairs = q_ref.shape[2] // LANES
    i = pl.program_id(1)
    qaux = qaux_ref[0]
    for p in range(pairs):
        qs_ref[p] = _stacked_queries(q_ref[0, :, p * LANES:(p + 1) * LANES], qaux, p)
    m_ref[...] = jnp.full(m_ref.shape, NEG, F32)
    l_ref[...] = jnp.zeros(l_ref.shape, F32)
    acc_ref[...] = jnp.zeros(acc_ref.shape, F32)

    def tile(j, mask):
        start = pl.multiple_of(j * t, t)
        kaux = kaux_ref[0, pl.ds(start, t), :]
        scores = []
        for p in range(pairs):
            kt = jnp.concatenate([k_ref[0, pl.ds(start, t), p * LANES:(p + 1) * LANES], kaux], axis=1)
            scores.append(_dot_nt(kt, qs_ref[p]))
        probs, alphas = [], []
        for p in range(pairs):
            s = scores[p] if mask is None else jnp.where(mask, scores[p], NEG)
            m = m_ref[p]
            m_new = jnp.maximum(m, jnp.max(s, axis=0, keepdims=True))
            alpha = jnp.exp2(m - m_new)
            pe = jnp.exp2(s - m_new)
            l_ref[p] = alpha * l_ref[p] + jnp.sum(pe, axis=0, keepdims=True)
            m_ref[p] = m_new
            probs.append(pe.astype(BF16))
            alphas.append(alpha)
        for p in range(pairs):
            pv = _dot(vt_ref[0, p * LANES:(p + 1) * LANES, pl.ds(start, t)], probs[p])
            acc_ref[p] = alphas[p] * acc_ref[p] + pv

    def body(j, carry):
        tile(j, None)
        return carry

    lax.fori_loop(0, i, body, 0)
    key = lax.broadcasted_iota(jnp.int32, (t, 2 * t), 0)
    query = lax.broadcasted_iota(jnp.int32, (t, 2 * t), 1)
    query = jnp.where(query >= t, query - t, query)
    tile(i, key <= query)
    row = lax.broadcasted_iota(jnp.int32, (LANES, t), 0)
    for p in range(pairs):
        out = acc_ref[p] / l_ref[p]
        o_ref[0, :, p * LANES:(p + 1) * LANES] = jnp.where(row < HEAD_DIM, out[:, :t], out[:, t:]).T.astype(BF16)


def _fox_attention(q, qaux, kb, kaux, vt):
    bsz, s, d = q.shape
    t = FOX_TILE
    pairs = d // LANES
    return pl.pallas_call(
        _fox_attn_kernel,
        grid=(bsz, s // t),
        in_specs=[pl.BlockSpec((1, t, d), lambda b, i: (b, i, 0)),
                  pl.BlockSpec((1, t, LANES), lambda b, i: (b, i, 0)),
                  pl.BlockSpec((1, s, d), lambda b, i: (b, 0, 0)),
                  pl.BlockSpec((1, s, LANES), lambda b, i: (b, 0, 0)),
                  pl.BlockSpec((1, d, s), lambda b, i: (b, 0, 0))],
        out_specs=pl.BlockSpec((1, t, d), lambda b, i: (b, i, 0)),
        out_shape=jax.ShapeDtypeStruct((bsz, s, d), BF16),
        scratch_shapes=[pltpu.VMEM((pairs, 2 * t, 2 * LANES), BF16),
                        pltpu.VMEM((pairs, 1, 2 * t), F32),
                        pltpu.VMEM((pairs, 1, 2 * t), F32),
                        pltpu.VMEM((pairs, LANES, 2 * t), F32)],
        compiler_params=_params("arbitrary", "arbitrary"),
        name="fox_attention",
    )(q, qaux, kb, kaux, vt)


def _cache_suffix_kernel(lf_ref, later_ref, suf_ref, carry_ref):
    @pl.when(pl.program_id(1) == 0)
    def _():
        carry_ref[...] = jnp.zeros_like(carry_ref)

    lf = lf_ref[0]
    later = later_ref[...]
    l3 = _split3(lf)
    carry = carry_ref[...]
    suf_ref[0] = (_dot(l3[0], later) + _dot(l3[1], later) + _dot(l3[2], later) + carry) * LOG2E
    carry_ref[...] = carry + jnp.sum(lf, axis=1, keepdims=True)


def _cache_suffix(clogf_t, layer):
    _, bsz, n_heads, plen = clogf_t.shape
    t = FOX_SUFFIX_TILE
    nt = plen // t
    r = np.arange(t)
    later = (r[:, None] > r[None, :]).astype(np.float32)
    return pl.pallas_call(
        _cache_suffix_kernel,
        grid=(bsz, nt),
        in_specs=[pl.BlockSpec((None, 1, n_heads, t), lambda b, j: (layer, b, 0, nt - 1 - j)),
                  pl.BlockSpec((t, t), lambda b, j: (0, 0))],
        out_specs=pl.BlockSpec((1, n_heads, t), lambda b, j: (b, 0, nt - 1 - j)),
        out_shape=jax.ShapeDtypeStruct((bsz, n_heads, plen), F32),
        scratch_shapes=[pltpu.VMEM((n_heads, 1), F32)],
        compiler_params=_params("arbitrary", "arbitrary"),
        name="fox_cache_suffix",
    )(clogf_t, jnp.asarray(later, BF16))


def _fox_sample_kernel(q_ref, cum_ref, kt_ref, vt_ref, suf_ref, kn_ref, vn_ref, cumt_ref, o_ref, m_ref, l_ref, acc_ref):
    step = pl.program_id(1)
    n_heads, n_new = q_ref.shape[1], q_ref.shape[2]

    @pl.when(step == 0)
    def _():
        m_ref[...] = jnp.full(m_ref.shape, NEG, F32)
        l_ref[...] = jnp.zeros(l_ref.shape, F32)
        acc_ref[...] = jnp.zeros(acc_ref.shape, F32)

    cum = cum_ref[0]

    def update(scores, value_dot):
        probs, alphas = [], []
        for h in range(n_heads):
            s = scores[h]
            m_old = m_ref[h]
            m_new = jnp.maximum(m_old, jnp.max(s, axis=1, keepdims=True))
            alpha = jnp.exp2(m_old - m_new)
            pe = jnp.exp2(s - m_new)
            l_ref[h] = alpha * l_ref[h] + jnp.sum(pe, axis=1, keepdims=True)
            m_ref[h] = m_new
            probs.append(pe.astype(BF16))
            alphas.append(alpha)
        for h in range(n_heads):
            acc_ref[h] = alphas[h] * acc_ref[h] + value_dot(h, probs[h])

    update([_dot(q_ref[0, h], kt_ref[h].astype(BF16)) + cum[:, h:h + 1] + suf_ref[0, h:h + 1, :]
            for h in range(n_heads)],
           lambda h, pr: _dot_nt(pr, vt_ref[h].astype(BF16)))

    @pl.when(step == pl.num_programs(1) - 1)
    def _():
        row = lax.broadcasted_iota(jnp.int32, (n_new, n_new), 0)
        col = lax.broadcasted_iota(jnp.int32, (n_new, n_new), 1)
        update([jnp.where(col <= row,
                          _dot_nt(q_ref[0, h], kn_ref[0, h]) + cum[:, h:h + 1] - cumt_ref[0, h:h + 1, :], NEG)
                for h in range(n_heads)],
               lambda h, pr: _dot(pr, vn_ref[0, h]))
        for h in range(n_heads):
            o_ref[0, :, h * HEAD_DIM:(h + 1) * HEAD_DIM] = (acc_ref[h] / l_ref[h]).astype(BF16)


def _fox_sample_attention(q, cum, cache_kt, cache_vt, layer, suffix, kn, vn, cum_t):
    bsz, n_heads, n_new, _ = q.shape
    plen = cache_kt.shape[-1]
    t = FOX_CACHE_TILE
    heads = pl.BlockSpec((1, n_heads, n_new, HEAD_DIM), lambda b, j: (b, 0, 0, 0))
    cache = pl.BlockSpec((None, None, n_heads, HEAD_DIM, t), lambda b, j: (layer, b, 0, 0, j))
    return pl.pallas_call(
        _fox_sample_kernel,
        grid=(bsz, plen // t),
        in_specs=[heads, pl.BlockSpec((1, n_new, LANES), lambda b, j: (b, 0, 0)), cache, cache,
                  pl.BlockSpec((1, n_heads, t), lambda b, j: (b, 0, j)), heads, heads,
                  pl.BlockSpec((1, LANES, n_new), lambda b, j: (b, 0, 0))],
        out_specs=pl.BlockSpec((1, n_new, n_heads * HEAD_DIM), lambda b, j: (b, 0, 0)),
        out_shape=jax.ShapeDtypeStruct((bsz, n_new, n_heads * HEAD_DIM), BF16),
        scratch_shapes=[pltpu.VMEM((n_heads, n_new, 1), F32),
                        pltpu.VMEM((n_heads, n_new, 1), F32),
                        pltpu.VMEM((n_heads, n_new, HEAD_DIM), F32)],
        compiler_params=_params("arbitrary", "arbitrary"),
        name="fox_sample_attention",
    )(q, cum, cache_kt, cache_vt, suffix, kn, vn, cum_t)


def _rope_tables(positions):
    half = HEAD_DIM // 2
    inv = ROPE_THETA ** (-jnp.arange(half, dtype=F32) / half)
    ang = positions.astype(F32)[:, None] * inv[None, :]
    cos, sin = jnp.cos(ang), jnp.sin(ang)
    reps = LANES // HEAD_DIM
    return jnp.tile(jnp.concatenate([cos, cos], axis=1), (1, reps)), jnp.tile(jnp.concatenate([-sin, sin], axis=1), (1, reps))


def _rope(x, cos, sin_signed):
    n = x.shape[1]
    reps = n // LANES
    lane = lax.broadcasted_iota(jnp.int32, x.shape, 1)
    half = HEAD_DIM // 2
    partner = jnp.where((lane & (HEAD_DIM - 1)) < half, pltpu.roll(x, n - half, 1), pltpu.roll(x, half, 1))
    return x * jnp.concatenate([cos] * reps, axis=1) + partner * jnp.concatenate([sin_signed] * reps, axis=1)


def _inproj_swa_kernel(x_ref, sh_ref, sc_ref, w_ref, b_ref, cos_ref, sin_ref, q_ref, kb_ref, vb_ref, k_ref, v_ref):
    bb, lt, d = x_ref.shape
    tm = bb * lt
    nq = q_ref.shape[1] * HEAD_DIM
    nkv = kb_ref.shape[1] * HEAD_DIM
    h = (x_ref[...] * (1.0 + sc_ref[...]) + sh_ref[...]).reshape(tm, d).astype(BF16)
    p = _dot(h, w_ref[...]) + b_ref[...]
    cos = jnp.concatenate([cos_ref[...]] * bb, axis=0)
    sin = jnp.concatenate([sin_ref[...]] * bb, axis=0)
    q = _rope(p[:, :nq], cos, sin)
    k = _rope(p[:, nq:nq + nkv], cos, sin)
    v = p[:, nq + nkv:]
    k_ref[...] = k.reshape(bb, lt, nkv)
    v_ref[...] = v.reshape(bb, lt, nkv)
    qb = (q * ATT_SCALE).astype(BF16)
    kb = k.astype(BF16)
    vb = v.astype(BF16)
    for hd in range(nq // HEAD_DIM):
        q_ref[:, hd] = qb[:, hd * HEAD_DIM:(hd + 1) * HEAD_DIM].reshape(bb, lt, HEAD_DIM)
    for hd in range(nkv // HEAD_DIM):
        kb_ref[:, hd] = kb[:, hd * HEAD_DIM:(hd + 1) * HEAD_DIM].reshape(bb, lt, HEAD_DIM)
        vb_ref[:, hd] = vb[:, hd * HEAD_DIM:(hd + 1) * HEAD_DIM].reshape(bb, lt, HEAD_DIM)


def _inproj_swa(x, sh, sc, w, b, positions, n_q_heads):
    bsz, length, d = x.shape
    n = w.shape[1]
    n_kv_heads = n_q_heads // SWA_GROUP
    nkv = n_kv_heads * HEAD_DIM
    bb, lt = _row_tiles(bsz, length)
    cos, sin = _rope_tables(positions)
    const = lambda shape: pl.BlockSpec(shape, lambda bi, i: (0,) * len(shape))
    mod = pl.BlockSpec((bb, 1, d), lambda bi, i: (bi, 0, 0))
    tab = pl.BlockSpec((lt, LANES), lambda bi, i: (i, 0))
    heads = lambda nh: pl.BlockSpec((bb, nh, lt, HEAD_DIM), lambda bi, i: (bi, 0, i, 0))
    rows = lambda wd: pl.BlockSpec((bb, lt, wd), lambda bi, i: (bi, i, 0))
    return pl.pallas_call(
        _inproj_swa_kernel,
        grid=(bsz // bb, length // lt),
        in_specs=[rows(d), mod, mod, const((d, n)), const((1, n)), tab, tab],
        out_specs=[heads(n_q_heads), heads(n_kv_heads), heads(n_kv_heads), rows(nkv), rows(nkv)],
        out_shape=[jax.ShapeDtypeStruct((bsz, n_q_heads, length, HEAD_DIM), BF16),
                   jax.ShapeDtypeStruct((bsz, n_kv_heads, length, HEAD_DIM), BF16),
                   jax.ShapeDtypeStruct((bsz, n_kv_heads, length, HEAD_DIM), BF16),
                   jax.ShapeDtypeStruct((bsz, length, nkv), F32),
                   jax.ShapeDtypeStruct((bsz, length, nkv), F32)],
        compiler_params=_params("arbitrary", "arbitrary"),
        name="inproj_swa",
    )(x, sh, sc, w, b, cos, sin)


def _swa_attn_kernel(sinks_ref, q_ref, k_ref, v_ref, o_ref, *, q_offset):
    g, tq, _ = q_ref.shape[1:]
    wn = tq + WINDOW
    kvh = pl.program_id(1)
    q0 = q_offset + pl.program_id(2) * tq
    ws = pl.multiple_of(jnp.maximum(q0 - WINDOW, 0), CHUNK)
    kw = k_ref[0, 0, pl.ds(ws, wn), :]
    vw = v_ref[0, 0, pl.ds(ws, wn), :]
    shift = CHUNK.bit_length() - 1
    qc = jnp.right_shift(q0 + lax.broadcasted_iota(jnp.int32, (tq, wn), 0), shift)
    kc = jnp.right_shift(ws + lax.broadcasted_iota(jnp.int32, (tq, wn), 1), shift)
    visible = (kc <= qc) & (kc >= qc - WINDOW // CHUNK)
    scores = [_dot_nt(q_ref[0, i], kw) for i in range(g)]
    probs, denoms = [], []
    for i in range(g):
        sink = sinks_ref[kvh * g + i]
        s = jnp.where(visible, scores[i], NEG)
        m = jnp.maximum(jnp.max(s, axis=1, keepdims=True), sink)
        pe = jnp.exp(s - m)
        denoms.append(jnp.sum(pe, axis=1, keepdims=True) + jnp.exp(sink - m))
        probs.append(pe.astype(BF16))
    for i in range(g):
        o = _dot(probs[i], vw) / denoms[i]
        o_ref[0, :, i * HEAD_DIM:(i + 1) * HEAD_DIM] = o.astype(BF16)


def _swa_attention(q, k, v, sinks, q_offset):
    bsz, nh, lq, _ = q.shape
    nkv, lk = k.shape[1], k.shape[2]
    g = nh // nkv
    tq = min(SWA_TILE, lq)
    kv = pl.BlockSpec((1, 1, lk, HEAD_DIM), lambda b, h, i, s: (b, h, 0, 0))
    return pl.pallas_call(
        functools.partial(_swa_attn_kernel, q_offset=q_offset),
        grid_spec=pltpu.PrefetchScalarGridSpec(
            num_scalar_prefetch=1,
            grid=(bsz, nkv, lq // tq),
            in_specs=[pl.BlockSpec((1, g, tq, HEAD_DIM), lambda b, h, i, s: (b, h, i, 0)), kv, kv],
            out_specs=pl.BlockSpec((1, tq, g * HEAD_DIM), lambda b, h, i, s: (b, i, h))),
        out_shape=jax.ShapeDtypeStruct((bsz, lq, nh * HEAD_DIM), BF16),
        compiler_params=_params("arbitrary", "arbitrary", "arbitrary"),
        name="swa_attention",
    )(sinks, q, k, v)


def _route(r):
    lane = lax.broadcasted_iota(jnp.int32, r.shape, 1).astype(F32)
    big = float(LANES)
    gl = jnp.where(lane < N_GROUPS, r, NEG)
    gmax = jnp.max(gl, axis=1, keepdims=True)
    gp = 1.0 / jnp.sum(jnp.exp(gl - gmax), axis=1, keepdims=True)
    gi = jnp.min(jnp.where(gl == gmax, lane, big), axis=1, keepdims=True)
    lo = ROUTE_GATE_LANE + EXPERTS_PER_GROUP * gi
    el = jnp.where((lane >= lo) & (lane < lo + EXPERTS_PER_GROUP), r, NEG)
    e1 = jnp.max(el, axis=1, keepdims=True)
    esum = jnp.sum(jnp.exp(el - e1), axis=1, keepdims=True)
    i1 = jnp.min(jnp.where(el == e1, lane, big), axis=1, keepdims=True)
    el2 = jnp.where(lane == i1, NEG, el)
    e2 = jnp.max(el2, axis=1, keepdims=True)
    i2 = jnp.min(jnp.where(el2 == e2, lane, big), axis=1, keepdims=True)
    p1 = 1.0 / esum
    p2 = jnp.exp(e2 - e1) / esum
    w1 = gp * p1 / (p1 + p2)
    w2 = gp * p2 / (p1 + p2)
    gates = jnp.where(lane == i1, w1, jnp.where(lane == i2, w2, 0.0))
    return jnp.where(lane == ROUTE_GROUP_LANE, gi, gates)


def _outproj_kernel(x_ref, o_ref, w_ref, g1_ref, lng_ref, lnb_ref, sh_ref, sc_ref, wrh_ref, wrl_ref, br_ref,
                    x1_ref, h2_ref, route_ref, *, alpha):
    bb, lt, d = x_ref.shape
    tm = bb * lt
    a = _dot(o_ref[...].reshape(tm, d), w_ref[...]).reshape(bb, lt, d)
    x1 = _layer_norm(alpha * x_ref[...] + (1.0 + g1_ref[...]) * a, lng_ref[...], lnb_ref[...])
    x1_ref[...] = x1
    h2 = (x1 * (1.0 + sc_ref[...]) + sh_ref[...]).reshape(tm, d)
    hi = h2.astype(BF16)
    lo = (h2 - hi.astype(F32)).astype(BF16)
    h2_ref[...] = hi.reshape(bb, lt, d)
    r = _dot(hi, wrh_ref[...]) + _dot(hi, wrl_ref[...]) + _dot(lo, wrh_ref[...]) + br_ref[...]
    route_ref[...] = _route(r).reshape(bb, lt, LANES)


def _outproj(x, o, w_out, g1, lng, lnb, sh2, sc2, wr_hi, wr_lo, br, alpha):
    bsz, length, d = x.shape
    bb, lt = _row_tiles(bsz, length)
    const = lambda shape: pl.BlockSpec(shape, lambda b, i: (0,) * len(shape))
    rows = lambda w: pl.BlockSpec((bb, lt, w), lambda b, i: (b, i, 0))
    mod = pl.BlockSpec((bb, 1, d), lambda b, i: (b, 0, 0))
    return pl.pallas_call(
        functools.partial(_outproj_kernel, alpha=alpha),
        grid=(bsz // bb, length // lt),
        in_specs=[rows(d), rows(d), const((d, d)), mod, const((1, d)), const((1, d)), mod, mod,
                  const((d, LANES)), const((d, LANES)), const((1, LANES))],
        out_specs=[rows(d), rows(d), rows(LANES)],
        out_shape=[jax.ShapeDtypeStruct((bsz, length, d), F32), jax.ShapeDtypeStruct((bsz, length, d), BF16),
                   jax.ShapeDtypeStruct((bsz, length, LANES), F32)],
        compiler_params=_params("arbitrary", "arbitrary"),
        name="outproj_router",
    )(x, o, w_out, g1, lng, lnb, sh2, sc2, wr_hi, wr_lo, br)


def _moe_kernel(h_ref, route_ref, wg_ref, wu_ref, wd_ref, x1_ref, g2_ref, lng_ref, lnb_ref, x2_ref, acc_ref,
                *, alpha):
    bb, lt, d = h_ref.shape
    tm = bb * lt
    grp = pl.program_id(2)

    @pl.when(grp == 0)
    def _():
        acc_ref[...] = jnp.zeros_like(acc_ref)

    h = h_ref[...].reshape(tm, d)
    route = route_ref[...].reshape(tm, LANES)
    lane = lax.broadcasted_iota(jnp.int32, route.shape, 1)
    base = ROUTE_GATE_LANE + EXPERTS_PER_GROUP * grp
    acts = []
    for e in range(EXPERTS_PER_GROUP):
        gate = jnp.sum(jnp.where(lane == base + e, route, 0.0), axis=1, keepdims=True)
        hg = _dot(h, wg_ref[e])
        hu = _dot(h, wu_ref[e])
        acts.append(((hg / (1.0 + jnp.exp(-hg))) * hu * gate).astype(BF16))
    a = jnp.concatenate(acts, axis=1)
    acc_ref[...] += _dot(a, wd_ref[...].reshape(a.shape[1], d))

    @pl.when(grp == pl.num_programs(2) - 1)
    def _():
        y = alpha * x1_ref[...] + (1.0 + g2_ref[...]) * acc_ref[...].reshape(bb, lt, d)
        x2_ref[...] = _layer_norm(y, lng_ref[...], lnb_ref[...])


def _moe(h2, route, wg, wu, wd, layer, x1, g2, lng, lnb, alpha):
    bsz, length, d = h2.shape
    f = wg.shape[-1]
    bb, lt = _row_tiles(bsz, length)
    const = lambda shape: pl.BlockSpec(shape, lambda b, i, g: (0,) * len(shape))
    rows = lambda w: pl.BlockSpec((bb, lt, w), lambda b, i, g: (b, i, 0))
    mod = pl.BlockSpec((bb, 1, d), lambda b, i, g: (b, 0, 0))
    epg = EXPERTS_PER_GROUP
    return pl.pallas_call(
        functools.partial(_moe_kernel, alpha=alpha),
        grid=(bsz // bb, length // lt, N_GROUPS),
        in_specs=[rows(d), rows(LANES),
                  pl.BlockSpec((None, epg, d, f), lambda b, i, g: (layer, g, 0, 0)),
                  pl.BlockSpec((None, epg, d, f), lambda b, i, g: (layer, g, 0, 0)),
                  pl.BlockSpec((None, epg, f, d), lambda b, i, g: (layer, g, 0, 0)),
                  rows(d), mod, const((1, d)), const((1, d))],
        out_specs=rows(d),
        out_shape=jax.ShapeDtypeStruct((bsz, length, d), F32),
        scratch_shapes=[pltpu.VMEM((bb * lt, d), F32)],
        compiler_params=_params("arbitrary", "arbitrary", "arbitrary"),
        name="moe_deepnorm",
    )(h2, route, wg, wu, wd, x1, g2, lng, lnb)


def kernel(x_prompt, x_sample, cache_fox_k, cache_fox_v, cache_fox_logf, cache_swa_k, cache_swa_v, c_prompt, c_sample, mod_w, mod_b, ln1_g, ln1_b, ln2_g, ln2_b, fox_w_in, fox_b_in, fox_w_out, swa_w_in, swa_b_in, swa_sinks, swa_w_out, router_g_w, router_g_b, router_e_w, router_e_b, w_gate, w_up, w_down):
    depth = mod_w.shape[0]
    bp, seq, d = x_prompt.shape
    bs, dec = x_sample.shape[:2]
    past = cache_fox_k.shape[2]
    fox_heads = cache_fox_k.shape[3]
    swa_kv = cache_swa_k.shape[3]
    swa_heads = swa_kv * SWA_GROUP
    keep = cache_swa_k.shape[2]
    alpha = (2.0 * depth) ** 0.25

    n_c = bp + bs
    c_all = jnp.pad(jnp.concatenate([c_prompt, c_sample], axis=0), ((0, -n_c % 16), (0, 0)))
    mods = _modulation(c_all, mod_w, mod_b)

    def mod_parts(l, lo, hi):
        return [mods[l, lo:hi, i * d:(i + 1) * d].reshape(hi - lo, 1, d) for i in range(6)]

    ck_t = jnp.transpose(cache_fox_k, (0, 1, 3, 4, 2))
    cv_t = jnp.transpose(cache_fox_v, (0, 1, 3, 4, 2))
    clogf_t = jnp.transpose(cache_fox_logf, (0, 1, 3, 2))
    wg_b, wu_b, wd_b = w_gate.astype(BF16), w_up.astype(BF16), w_down.astype(BF16)
    n_grp = router_g_w.shape[-1]
    n_exp = w_gate.shape[1]

    xp, xs = x_prompt, x_sample
    fkp, fvp, flp, skp, svp = [], [], [], [], []
    fks, fvs, fls, sks, svs = [], [], [], [], []
    for l in range(depth):
        j = l // 2
        mp = mod_parts(l, 0, bp)
        ms = mod_parts(l, bp, n_c)
        if l % 2 == 0:
            n_qkv = 3 * d
            wqkv = fox_w_in[j, :, :n_qkv].astype(BF16)
            bqkv = fox_b_in[j, :n_qkv].reshape(1, n_qkv)
            wf = jnp.pad(fox_w_in[j, :, n_qkv:], ((0, 0), (0, LANES - fox_heads))).astype(BF16)
            bf_ = jnp.pad(fox_b_in[j, n_qkv:], (0, LANES - fox_heads)).reshape(1, LANES)
            k_p, v_p, lf_p, q, kb, vt, qaux, kaux = _inproj_fox(xp, mp[0], mp[1], wqkv, bqkv, wf, bf_, fox_heads, False)
            op = _fox_attention(q, qaux, kb, kaux, vt)
            k_s, v_s, lf_s, q, kb, vb, cum, cum_t = _inproj_fox(xs, ms[0], ms[1], wqkv, bqkv, wf, bf_, fox_heads, True)
            suffix = _cache_suffix(clogf_t, j)
            os_ = _fox_sample_attention(q, cum, ck_t, cv_t, j, suffix, kb, vb, cum_t)
            hshape = (fox_heads, HEAD_DIM)
            fkp.append(k_p.reshape(bp, seq, *hshape)); fvp.append(v_p.reshape(bp, seq, *hshape)); flp.append(lf_p)
            fks.append(k_s.reshape(bs, dec, *hshape)); fvs.append(v_s.reshape(bs, dec, *hshape)); fls.append(lf_s)
            w_out = fox_w_out[j].astype(BF16)
        else:
            w_in = swa_w_in[j].astype(BF16)
            b_in = swa_b_in[j].reshape(1, -1)
            q, kb, vb, k_p, v_p = _inproj_swa(xp, mp[0], mp[1], w_in, b_in, jnp.arange(seq), swa_heads)
            op = _swa_attention(q, kb, vb, swa_sinks[j], 0)
            q, kb, vb, k_s, v_s = _inproj_swa(xs, ms[0], ms[1], w_in, b_in, past + jnp.arange(dec), swa_heads)
            hshape = (swa_kv, HEAD_DIM)
            k_all = jnp.concatenate([cache_swa_k[j], k_s.reshape(bs, dec, *hshape)], axis=1)
            v_all = jnp.concatenate([cache_swa_v[j], v_s.reshape(bs, dec, *hshape)], axis=1)
            os_ = _swa_attention(q, jnp.swapaxes(k_all, 1, 2).astype(BF16), jnp.swapaxes(v_all, 1, 2).astype(BF16),
                                 swa_sinks[j], keep)
            skp.append(k_p[:, seq - WINDOW:].reshape(bp, WINDOW, *hshape))
            svp.append(v_p[:, seq - WINDOW:].reshape(bp, WINDOW, *hshape))
            sks.append(k_all[:, k_all.shape[1] - keep:]); svs.append(v_all[:, v_all.shape[1] - keep:])
            w_out = swa_w_out[j].astype(BF16)

        wr = jnp.concatenate([router_g_w[l], jnp.moveaxis(router_e_w[l], 0, 1).reshape(d, n_exp)], axis=1)
        wr = jnp.pad(wr, ((0, 0), (0, LANES - n_grp - n_exp)))
        wr_hi = wr.astype(BF16)
        wr_lo = (wr - wr_hi.astype(F32)).astype(BF16)
        br = jnp.pad(jnp.concatenate([router_g_b[l], router_e_b[l].reshape(-1)]), (0, LANES - n_grp - n_exp))
        br = br.reshape(1, LANES)
        lng1, lnb1 = ln1_g[l].reshape(1, d), ln1_b[l].reshape(1, d)
        lng2, lnb2 = ln2_g[l].reshape(1, d), ln2_b[l].reshape(1, d)
        for grp_in in ("prompt", "sample"):
            x, o, m = (xp, op, mp) if grp_in == "prompt" else (xs, os_, ms)
            x1, h2, route = _outproj(x, o, w_out, m[2], lng1, lnb1, m[3], m[4], wr_hi, wr_lo, br, alpha)
            x2 = _moe(h2, route, wg_b, wu_b, wd_b, l, x1, m[5], lng2, lnb2, alpha)
            if grp_in == "prompt":
                xp = x2
            else:
                xs = x2
    return (xp, xs,
            jnp.stack(fkp), jnp.stack(fvp), jnp.stack(flp), jnp.stack(skp), jnp.stack(svp),
            jnp.stack(fks), jnp.stack(fvs), jnp.stack(fls), jnp.stack(sks), jnp.stack(svs))
```
